```python
import math
import jax
import jax.numpy as jnp
from jax import lax
import numpy as np

D_MODEL = 4096
BATCH = 4
SEQ = 2048
DEPTH = 2
DEC_BATCH = 128
DEC_SEQ = 1
PAST_LEN = 16384
PAGE_SIZE = 128

EPS = 1e-6
QBLOCK = 128
MIX_W = D_MODEL // 2
N_BRANCH = 3
SSM_HEADDIM = 64
SSM_HEADS = MIX_W // SSM_HEADDIM
SSM_INNER = SSM_HEADS * SSM_HEADDIM
SSM_GROUPS = 4
SSM_DSTATE = 128
SSM_CONV = 4
SSM_CHUNK = 128
SSM_CONV_DIM = SSM_INNER + 2 * SSM_GROUPS * SSM_DSTATE
MLA_HEADS = 16
MLA_NOPE = 128
MLA_ROPE = 64
MLA_QK = MLA_NOPE + MLA_ROPE
MLA_V = MIX_W // MLA_HEADS
MLA_Q_LORA = 768
MLA_KV_LORA = 256
ROPE_THETA = 10000.0
NSA_D = 64
NSA_HEADS = MIX_W // NSA_D
CMP_STRIDE = 16
CMP_BLOCK = 2 * CMP_STRIDE
CMP_HIDDEN = 256
SLC_BLOCK = 64
SLC_TOPN = 16
N_LOCAL = 2
FORCE_BONUS = 1e6
WINDOW = 512
N_EXPERTS = 16
N_GROUPS = 4
EXPERTS_PER_GROUP = N_EXPERTS // N_GROUPS
GROUP_SCORE_TOPK = 2
TOP_K = 2
D_FF = 768

IN_SPLITS = (SSM_INNER, SSM_CONV_DIM, SSM_HEADS, MLA_Q_LORA, MLA_KV_LORA, MLA_ROPE,
             NSA_HEADS * NSA_D, 6 * NSA_D, 3 * NSA_HEADS, N_BRANCH * D_MODEL)
IN_COLS = sum(IN_SPLITS)
IN_SPLIT_POINTS = tuple(int(s) for s in np.cumsum(IN_SPLITS)[:-1])

kernel_name = 'hybrid_ssd_mla_nsa_moe_step'


def rmsnorm(x, g):
    xf = x.astype(jnp.float32)
    y = xf * lax.rsqrt(jnp.mean(xf * xf, axis=-1, keepdims=True) + EPS)
    return y.astype(x.dtype) * g


def rope(x, pos):
    half = x.shape[-1] // 2
    freqs = ROPE_THETA ** (-jnp.arange(half, dtype=jnp.float32) / half)
    ang = pos.astype(jnp.float32)[:, None] * freqs[None, :]
    cos = jnp.cos(ang)[None, :, None, :].astype(x.dtype)
    sin = jnp.sin(ang)[None, :, None, :].astype(x.dtype)
    x1, x2 = x[..., :half], x[..., half:]
    return jnp.concatenate([x1 * cos - x2 * sin, x1 * sin + x2 * cos], axis=-1)


def masked_softmax(s, mask):
    s = jnp.where(mask, s.astype(jnp.float32), -jnp.inf)
    m = jnp.max(s, axis=-1, keepdims=True)
    m = jnp.where(jnp.isfinite(m), m, 0.0)
    p = jnp.exp(s - m)
    return p / jnp.maximum(jnp.sum(p, axis=-1, keepdims=True), 1e-30)


def attend_shared_kv(q, k, v, mask, scale):
    s = jnp.einsum('...qhd,...kd->...hqk', q, k) * scale
    p = masked_softmax(s, mask)
    return jnp.einsum('...hqk,...kd->...qhd', p.astype(v.dtype), v), p


def query_block(n):
    return QBLOCK if n % QBLOCK == 0 else n


def split_query_blocks(t, qb):
    b, n = t.shape[:2]
    return jnp.moveaxis(t.reshape((b, n // qb, qb) + t.shape[2:]), 1, 0)


def merge_query_blocks(t):
    t = jnp.moveaxis(t, 0, 1)
    return t.reshape((t.shape[0], t.shape[1] * t.shape[2]) + t.shape[3:])


def gather_pages(pool, page_table):
    g = pool[page_table]
    return g.reshape((g.shape[0], g.shape[1] * g.shape[2]) + g.shape[3:])


def causal_conv(u, buf, w, b):
    K = w.shape[0]
    L = u.shape[1]
    up = jnp.concatenate([buf.astype(u.dtype), u], axis=1)
    out = b + sum(w[k] * up[:, k:k + L] for k in range(K))
    return out, up[:, up.shape[1] - (K - 1):]


def ssd_scan(x, dt, A, Bm, Cm, h0):
    b, l = x.shape[:2]
    T = SSM_CHUNK if l % SSM_CHUNK == 0 else l
    nc = l // T
    chunk = lambda t: t.reshape((b, nc, T) + t.shape[2:])
    xdt = chunk(x * dt[..., None])
    Bc, Cc = chunk(Bm), chunk(Cm)
    a_cum = jnp.cumsum(chunk(dt * A), axis=2)
    ac = jnp.moveaxis(a_cum, 2, -1)
    causal = jnp.tril(jnp.ones((T, T), dtype=bool))
    decay = jnp.exp(jnp.where(causal, ac[..., :, None] - ac[..., None, :], -jnp.inf))
    y_diag = jnp.einsum('bcign,bcjgn,bcgrij,bcjgrp->bcigrp', Cc, Bc, decay, xdt)
    decay_end = jnp.exp(a_cum[:, :, -1:] - a_cum)
    chunk_states = jnp.einsum('bcjgn,bcjgr,bcjgrp->bcgrpn', Bc, decay_end, xdt)
    chunk_decay = jnp.exp(a_cum[:, :, -1])

    def step(h, inp):
        st, dec = inp
        return dec[..., None, None] * h + st, h

    h_final, h_prev = lax.scan(step, h0.astype(chunk_states.dtype),
                               (jnp.moveaxis(chunk_states, 1, 0), jnp.moveaxis(chunk_decay, 1, 0)))
    h_prev = jnp.moveaxis(h_prev, 0, 1)
    y_off = jnp.einsum('bcign,bcgrpn,bcigr->bcigrp', Cc, h_prev, jnp.exp(a_cum))
    return (y_diag + y_off).reshape(x.shape), h_final


def ssm_mixer(z, xbc, dt_raw, conv_buf, h0, conv_w, conv_b, dt_bias, a_log, d_skip, g_norm):
    b, l = z.shape[:2]
    R = SSM_HEADS // SSM_GROUPS
    xbc, new_buf = causal_conv(xbc, conv_buf, conv_w, conv_b)
    xbc = jax.nn.silu(xbc)
    xs, Bm, Cm = jnp.split(xbc, [SSM_INNER, SSM_INNER + SSM_GROUPS * SSM_DSTATE], axis=-1)
    xs = xs.reshape(b, l, SSM_GROUPS, R, SSM_HEADDIM)
    Bm = Bm.reshape(b, l, SSM_GROUPS, SSM_DSTATE)
    Cm = Cm.reshape(b, l, SSM_GROUPS, SSM_DSTATE)
    dt = jax.nn.softplus((dt_raw + dt_bias).astype(jnp.float32)).reshape(b, l, SSM_GROUPS, R)
    A = -jnp.exp(a_log.astype(jnp.float32)).reshape(SSM_GROUPS, R)
    y, h_final = ssd_scan(xs, dt, A, Bm, Cm, h0.reshape(b, SSM_GROUPS, R, SSM_HEADDIM, SSM_DSTATE))
    y = y + d_skip.reshape(SSM_GROUPS, R)[:, :, None] * xs
    y = rmsnorm(y.reshape(b, l, SSM_INNER).astype(z.dtype) * jax.nn.silu(z), g_norm)
    return y, h_final.reshape(b, SSM_HEADS, SSM_HEADDIM, SSM_DSTATE), new_buf


def mla_attend(q_lat, q_pe, ckv, kpe, q_pos, k_pos):
    qb = query_block(q_lat.shape[1])
    scale = MLA_QK ** -0.5

    def one(args):
        ql, qp, qpos = args
        s = jnp.einsum('bqhk,bsk->bhqs', ql, ckv) + jnp.einsum('bqhr,bsr->bhqs', qp, kpe)
        p = masked_softmax(s * scale, k_pos[None, :] <= qpos[:, None])
        return jnp.einsum('bhqs,bsk->bqhk', p.astype(ckv.dtype), ckv)

    out = lax.map(one, (split_query_blocks(q_lat, qb), split_query_blocks(q_pe, qb), q_pos.reshape(-1, qb)))
    return merge_query_blocks(out)


def mla_mixer(cq, ckv, kpe, pos, past_ckv, past_kpe, g_q_a, w_uq, g_q, g_kv_a, g_kpe, w_uk, w_uv):
    b, l = cq.shape[:2]
    q = (rmsnorm(cq, g_q_a) @ w_uq).reshape(b, l, MLA_HEADS, MLA_QK)
    q = rmsnorm(q, g_q)
    q_nope, q_pe = q[..., :MLA_NOPE], rope(q[..., MLA_NOPE:], pos)
    ckv = rmsnorm(ckv, g_kv_a)
    kpe = rope(rmsnorm(kpe, g_kpe)[:, :, None, :], pos)[:, :, 0, :]
    if past_ckv is None:
        keys_ckv, keys_kpe, k_pos = ckv, kpe, pos
    else:
        keys_ckv = jnp.concatenate([past_ckv.astype(ckv.dtype), ckv], axis=1)
        keys_kpe = jnp.concatenate([past_kpe.astype(kpe.dtype), kpe], axis=1)
        k_pos = jnp.arange(keys_ckv.shape[1], dtype=jnp.int32)
    q_lat = jnp.einsum('blhd,khd->blhk', q_nope, w_uk)
    o_lat = mla_attend(q_lat, q_pe, keys_ckv, keys_kpe, pos, k_pos)
    o = jnp.einsum('blhk,khv->blhv', o_lat, w_uv).reshape(b, l, MLA_HEADS * MLA_V)
    return o, ckv, kpe


def compress(rows, pos_emb, w1, w2):
    b, T = rows.shape[:2]
    nc = T // CMP_STRIDE
    chunks = rows[:, :nc * CMP_STRIDE].reshape(b, nc, CMP_STRIDE, 2, NSA_D)
    w1a, w1b = w1[:, :CMP_STRIDE], w1[:, CMP_STRIDE:]
    first = jnp.einsum('bnskd,ksde->bnke', chunks, w1a)[:, :-1]
    second = jnp.einsum('bnskd,ksde->bnke', chunks, w1b)[:, 1:]
    pos_term = jnp.einsum('skd,ksde->ke', pos_emb, w1)
    h = jax.nn.gelu(first + second + pos_term)
    out = jnp.einsum('bnke,ked->bnkd', h, w2)
    return out[:, :, 0], out[:, :, 1]


def slc_importance(p_cmp, n_slc):
    b, lq, n_cmp = p_cmp.shape
    R = SLC_BLOCK // CMP_STRIDE
    pp = jnp.pad(p_cmp, ((0, 0), (0, 0), (1, R * (n_slc + 1) - 1 - n_cmp))).reshape(b, lq, n_slc + 1, R)
    return pp[..., :-1, 0] + 2.0 * jnp.sum(pp[..., :-1, 1:], axis=-1) + pp[..., 1:, 0]


def select_blocks(imp, pos, n_slc):
    j = jnp.arange(n_slc, dtype=jnp.int32)[None, :]
    cur = (pos // SLC_BLOCK)[:, None]
    valid = j <= cur
    forced = (j == 0) | (j >= cur - (N_LOCAL - 1))
    score = jnp.where(valid, imp + jnp.where(forced, FORCE_BONUS, 0.0), -jnp.inf)
    vals, idx = lax.top_k(score, min(SLC_TOPN, n_slc))
    return idx, jnp.isfinite(vals)


def slc_attend(q, slc_blocks, pos, idx, sel_valid, scale):
    b, lq = q.shape[:2]
    qb = query_block(lq)
    n_sel = idx.shape[-1]
    offs = jnp.arange(SLC_BLOCK, dtype=jnp.int32)

    def one(args):
        qq, pp, ii, vv = args
        g = jax.vmap(lambda blk, ix: blk[ix])(slc_blocks, ii)
        g = g.reshape(b, qb, n_sel * SLC_BLOCK, 2, NSA_D)
        kpos = (ii[..., None] * SLC_BLOCK + offs).reshape(b, qb, n_sel * SLC_BLOCK)
        mask = jnp.repeat(vv, SLC_BLOCK, axis=-1) & (kpos <= pp[None, :, None])
        o, _ = attend_shared_kv(qq[:, :, None], g[:, :, :, 0], g[:, :, :, 1], mask[:, :, None, None, :], scale)
        return o[:, :, 0]

    out = lax.map(one, (split_query_blocks(q, qb), pos.reshape(-1, qb),
                        split_query_blocks(idx, qb), split_query_blocks(sel_valid, qb)))
    return merge_query_blocks(out)


def window_attend_banded(q, k, v, pos, scale):
    b, l = q.shape[:2]
    qb = query_block(l)
    nb = l // qb
    nw = -(-WINDOW // qb)
    pad = lambda t: jnp.pad(t, ((0, 0), (nw * qb, 0), (0, 0))).reshape(b, nw + nb, qb, NSA_D)
    band = jnp.arange(nb)[:, None] + jnp.arange(nw + 1)[None, :]
    kb = pad(k)[:, band].reshape(b, nb, (nw + 1) * qb, NSA_D)
    vb = pad(v)[:, band].reshape(b, nb, (nw + 1) * qb, NSA_D)
    kpos = (band[..., None] * qb + jnp.arange(qb)).reshape(nb, -1) - nw * qb
    dist = pos.reshape(nb, qb)[:, :, None] - kpos[:, None, :]
    mask = (dist >= 0) & (dist < WINDOW) & (kpos[:, None, :] >= 0)
    o, _ = attend_shared_kv(q.reshape(b, nb, qb, NSA_HEADS, NSA_D), kb, vb, mask[None, :, None], scale)
    return o.reshape(b, l, NSA_HEADS, NSA_D)


def nsa_mixer(q, kv, gates, pos, past_cmp, past_slc, win_buf, g_q, g_k, cmp_pos, cmp_w1, cmp_w2):
    b, l = q.shape[:2]
    scale = NSA_D ** -0.5
    q = rmsnorm(q.reshape(b, l, NSA_HEADS, NSA_D), g_q)
    kv = kv.reshape(b, l, 3, 2, NSA_D)
    cmp_rows = kv[:, :, 0]
    slc_rows = jnp.stack([rmsnorm(kv[:, :, 1, 0], g_k[1]), kv[:, :, 1, 1]], axis=2)
    win_rows = jnp.stack([rmsnorm(kv[:, :, 2, 0], g_k[2]), kv[:, :, 2, 1]], axis=2)
    if past_cmp is None:
        cmp_all, slc_all = cmp_rows, slc_rows
    else:
        cmp_all = jnp.concatenate([past_cmp.astype(cmp_rows.dtype), cmp_rows], axis=1)
        slc_all = jnp.concatenate([past_slc.astype(slc_rows.dtype), slc_rows], axis=1)
    kc, vc = compress(cmp_all, cmp_pos, cmp_w1, cmp_w2)
    kc = rmsnorm(kc, g_k[0])
    cmp_end = jnp.arange(kc.shape[1], dtype=jnp.int32) * CMP_STRIDE + CMP_BLOCK - 1
    o_cmp, p_cmp = attend_shared_kv(q, kc, vc, (cmp_end[None, :] <= pos[:, None])[None, None], scale)
    T = slc_all.shape[1]
    n_slc = -(-T // SLC_BLOCK)
    imp = slc_importance(jnp.sum(p_cmp, axis=1), n_slc)
    idx, sel_valid = select_blocks(imp, pos, n_slc)
    slc_pad = jnp.pad(slc_all, ((0, 0), (0, n_slc * SLC_BLOCK - T), (0, 0), (0, 0)))
    o_slc = slc_attend(q, slc_pad.reshape(b, n_slc, SLC_BLOCK, 2, NSA_D), pos, idx, sel_valid, scale)
    if win_buf is None:
        o_win = window_attend_banded(q, win_rows[:, :, 0], win_rows[:, :, 1], pos, scale)
        new_win = win_rows[:, max(l - WINDOW, 0):]
    else:
        wb = win_buf.shape[1]
        rows = jnp.concatenate([win_buf.astype(win_rows.dtype), win_rows], axis=1)
        kpos = jnp.arange(rows.shape[1], dtype=jnp.int32) + (pos[0] - wb)
        dist = pos[:, None] - kpos[None, :]
        mask = (dist >= 0) & (dist < WINDOW)
        o_win, _ = attend_shared_kv(q, rows[:, :, 0], rows[:, :, 1], mask[None, None], scale)
        new_win = rows[:, rows.shape[1] - wb:]
    g = jax.nn.sigmoid(gates.reshape(b, l, 3, NSA_HEADS).astype(jnp.float32))[..., None]
    o = g[:, :, 0] * o_cmp + g[:, :, 1] * o_slc + g[:, :, 2] * o_win
    return o.reshape(b, l, NSA_HEADS * NSA_D).astype(q.dtype), cmp_rows, slc_rows, new_win


def moe_ffn(h, w_router, router_bias, w_gu, w_down):
    shp = h.shape
    t = h.reshape(-1, shp[-1])
    scores = jax.nn.sigmoid((t @ w_router).astype(jnp.float32))
    biased = scores + router_bias
    grp = jnp.sum(lax.top_k(biased.reshape(-1, N_GROUPS, EXPERTS_PER_GROUP), GROUP_SCORE_TOPK)[0], axis=-1)
    best = jnp.argmax(grp, axis=-1)
    in_grp = (jnp.arange(N_EXPERTS) // EXPERTS_PER_GROUP)[None, :] == best[:, None]
    _, idx = lax.top_k(jnp.where(in_grp, biased, -jnp.inf), TOP_K)
    w = jnp.take_along_axis(scores, idx, axis=-1)
    w = w / jnp.sum(w, axis=-1, keepdims=True)
    gate = jnp.sum(jax.nn.one_hot(idx, N_EXPERTS, dtype=w.dtype) * w[..., None], axis=1)
    gu = jnp.einsum('td,edf->tef', t, w_gu)
    a = jax.nn.silu(gu[..., :D_FF]) * gu[..., D_FF:] * gate[..., None].astype(t.dtype)
    return jnp.einsum('tef,efd->td', a, w_down).reshape(shp)


def decoder_layer(x, c, pos, past, p):
    b, l, _ = x.shape
    mod = (jax.nn.silu(c) @ p['w_ada'] + p['b_ada'])[:, None, :]
    sh1, sc1, ga1, sh2, sc2, ga2 = jnp.split(mod, 6, axis=-1)
    hn = rmsnorm(x, p['g_norm_mix']) * (1.0 + sc1) + sh1
    z, xbc, dt_raw, cq, ckv, kpe, nq, nkv, ngate, bgate = jnp.split(hn @ p['w_in'], IN_SPLIT_POINTS, axis=-1)
    if past is None:
        past_ckv = past_kpe = past_cmp = past_slc = win_buf = None
        h0 = jnp.zeros((b, SSM_HEADS, SSM_HEADDIM, SSM_DSTATE), jnp.float32)
        conv_buf = jnp.zeros((b, SSM_CONV - 1, SSM_CONV_DIM), x.dtype)
    else:
        past_ckv, past_kpe, past_cmp, past_slc, win_buf, h0, conv_buf = past
    o_ssm, h_new, conv_new = ssm_mixer(z, xbc, dt_raw, conv_buf, h0, p['ssm_conv_w'], p['ssm_conv_b'],
                                       p['ssm_dt_bias'], p['ssm_a_log'], p['ssm_d'], p['ssm_g_norm'])
    o_mla, ckv_new, kpe_new = mla_mixer(cq, ckv, kpe, pos, past_ckv, past_kpe, p['mla_g_q_a'], p['mla_w_uq'],
                                        p['mla_g_q'], p['mla_g_kv_a'], p['mla_g_kpe'], p['mla_w_uk'], p['mla_w_uv'])
    o_nsa, cmp_new, slc_new, win_new = nsa_mixer(nq, nkv, ngate, pos, past_cmp, past_slc, win_buf, p['nsa_g_q'],
                                                 p['nsa_g_k'], p['nsa_cmp_pos'], p['nsa_cmp_w1'], p['nsa_cmp_w2'])
    branches = jnp.stack([o_ssm, o_mla, o_nsa], axis=2).astype(x.dtype)
    proj = jnp.einsum('blmk,mkd->blmd', branches, p['w_branch'])
    gates = jax.nn.sigmoid(bgate.reshape(b, l, N_BRANCH, D_MODEL))
    mix = jnp.sum(gates * proj, axis=2) @ p['w_out']
    x = x + ga1 * mix
    h2 = rmsnorm(x, p['g_norm_ffn']) * (1.0 + sc2) + sh2
    x = x + ga2 * moe_ffn(h2, p['w_router'], p['router_bias'], p['moe_w_gu'], p['moe_w_down'])
    return x, (ckv_new, kpe_new, cmp_new, slc_new, win_new, h_new, conv_new)


def setup_inputs(seed: int = 0) -> dict:
    key = jax.random.key(seed)
    ks = iter(jax.random.split(key, 64))

    def nrm(shape, scale=1.0):
        return jax.random.normal(next(ks), shape, jnp.float32) * scale

    def gain(shape):
        return 1.0 + nrm(shape, 0.02)

    n_pages = PAST_LEN // PAGE_SIZE
    n_used = DEC_BATCH * n_pages
    n_pool = n_used + max(1, n_used // 4)
    win_buf = min(WINDOW, PAST_LEN)
    page_table = jax.random.permutation(next(ks), n_pool)[:n_used].reshape(DEC_BATCH, n_pages).astype(jnp.int32)
    dt0 = jnp.exp(jax.random.uniform(next(ks), (DEPTH, SSM_HEADS), jnp.float32, math.log(1e-3), math.log(1e-1)))
    a0 = jax.random.uniform(next(ks), (DEPTH, SSM_HEADS), jnp.float32, 1.0, 16.0)
    return {
        'x_prompt': nrm((BATCH, SEQ, D_MODEL)),
        'x_sample': nrm((DEC_BATCH, DEC_SEQ, D_MODEL)),
        'cache_mla_ckv': nrm((DEPTH, n_pool, PAGE_SIZE, MLA_KV_LORA)),
        'cache_mla_kpe': nrm((DEPTH, n_pool, PAGE_SIZE, MLA_ROPE)),
        'cache_nsa_cmp': nrm((DEPTH, n_pool, PAGE_SIZE, 2, NSA_D)),
        'cache_nsa_slc': nrm((DEPTH, n_pool, PAGE_SIZE, 2, NSA_D)),
        'state_nsa_win': nrm((DEPTH, DEC_BATCH, win_buf, 2, NSA_D)),
        'state_ssm': nrm((DEPTH, DEC_BATCH, SSM_HEADS, SSM_HEADDIM, SSM_DSTATE), 0.1),
        'state_conv': nrm((DEPTH, DEC_BATCH, SSM_CONV - 1, SSM_CONV_DIM)),
        'page_table': page_table,
        'c_prompt': nrm((BATCH, D_MODEL)),
        'c_sample': nrm((DEC_BATCH, D_MODEL)),
        'w_ada': nrm((DEPTH, D_MODEL, 6 * D_MODEL), 0.5 * D_MODEL ** -0.5),
        'b_ada': nrm((DEPTH, 6 * D_MODEL), 0.02),
        'g_norm_mix': gain((DEPTH, D_MODEL)),
        'g_norm_ffn': gain((DEPTH, D_MODEL)),
        'w_in': nrm((DEPTH, D_MODEL, IN_COLS), D_MODEL ** -0.5),
        'ssm_conv_w': nrm((DEPTH, SSM_CONV, SSM_CONV_DIM), SSM_CONV ** -0.5),
        'ssm_conv_b': nrm((DEPTH, SSM_CONV_DIM), 0.02),
        'ssm_dt_bias': dt0 + jnp.log(-jnp.expm1(-dt0)),
        'ssm_a_log': jnp.log(a0),
        'ssm_d': 1.0 + nrm((DEPTH, SSM_HEADS), 0.1),
        'ssm_g_norm': gain((DEPTH, SSM_INNER)),
        'mla_g_q_a': gain((DEPTH, MLA_Q_LORA)),
        'mla_w_uq': nrm((DEPTH, MLA_Q_LORA, MLA_HEADS * MLA_QK), MLA_Q_LORA ** -0.5),
        'mla_g_q': gain((DEPTH, MLA_QK)),
        'mla_g_kv_a': gain((DEPTH, MLA_KV_LORA)),
        'mla_g_kpe': gain((DEPTH, MLA_ROPE)),
        'mla_w_uk': nrm((DEPTH, MLA_KV_LORA, MLA_HEADS, MLA_NOPE), MLA_KV_LORA ** -0.5),
        'mla_w_uv': nrm((DEPTH, MLA_KV_LORA, MLA_HEADS, MLA_V), MLA_KV_LORA ** -0.5),
        'nsa_g_q': gain((DEPTH, NSA_D)),
        'nsa_g_k': gain((DEPTH, 3, NSA_D)),
        'nsa_cmp_pos': nrm((DEPTH, CMP_BLOCK, 2, NSA_D), 0.1),
        'nsa_cmp_w1': nrm((DEPTH, 2, CMP_BLOCK, NSA_D, CMP_HIDDEN), (CMP_BLOCK * NSA_D) ** -0.5),
        'nsa_cmp_w2': nrm((DEPTH, 2, CMP_HIDDEN, NSA_D), CMP_HIDDEN ** -0.5),
        'w_branch': nrm((DEPTH, N_BRANCH, MIX_W, D_MODEL), MIX_W ** -0.5),
        'w_out': nrm((DEPTH, D_MODEL, D_MODEL), D_MODEL ** -0.5),
        'moe_w_gu': nrm((DEPTH, N_EXPERTS, D_MODEL, 2 * D_FF), D_MODEL ** -0.5),
        'moe_w_down': nrm((DEPTH, N_EXPERTS, D_FF, D_MODEL), D_FF ** -0.5),
        'w_router': nrm((D_MODEL, N_EXPERTS), D_MODEL ** -0.5),
        'router_bias': nrm((N_EXPERTS,), 0.01),
    }


def reference(x_prompt, x_sample, cache_mla_ckv, cache_mla_kpe, cache_nsa_cmp, cache_nsa_slc, state_nsa_win,
              state_ssm, state_conv, page_table, c_prompt, c_sample, w_ada, b_ada, g_norm_mix, g_norm_ffn, w_in,
              ssm_conv_w, ssm_conv_b, ssm_dt_bias, ssm_a_log, ssm_d, ssm_g_norm, mla_g_q_a, mla_w_uq, mla_g_q,
              mla_g_kv_a, mla_g_kpe, mla_w_uk, mla_w_uv, nsa_g_q, nsa_g_k, nsa_cmp_pos, nsa_cmp_w1, nsa_cmp_w2,
              w_branch, w_out, moe_w_gu, moe_w_down, w_router, router_bias):
    past_len = page_table.shape[1] * cache_mla_ckv.shape[2]
    pos_p = jnp.arange(x_prompt.shape[1], dtype=jnp.int32)
    pos_s = past_len + jnp.arange(x_sample.shape[1], dtype=jnp.int32)
    xp, xs = x_prompt, x_sample
    sp, ss = [], []
    for i in range(DEPTH):
        p = {
            'w_ada': w_ada[i], 'b_ada': b_ada[i], 'g_norm_mix': g_norm_mix[i], 'g_norm_ffn': g_norm_ffn[i],
            'w_in': w_in[i], 'ssm_conv_w': ssm_conv_w[i], 'ssm_conv_b': ssm_conv_b[i],
            'ssm_dt_bias': ssm_dt_bias[i], 'ssm_a_log': ssm_a_log[i], 'ssm_d': ssm_d[i],
            'ssm_g_norm': ssm_g_norm[i], 'mla_g_q_a': mla_g_q_a[i], 'mla_w_uq': mla_w_uq[i],
            'mla_g_q': mla_g_q[i], 'mla_g_kv_a': mla_g_kv_a[i], 'mla_g_kpe': mla_g_kpe[i],
            'mla_w_uk': mla_w_uk[i], 'mla_w_uv': mla_w_uv[i], 'nsa_g_q': nsa_g_q[i], 'nsa_g_k': nsa_g_k[i],
            'nsa_cmp_pos': nsa_cmp_pos[i], 'nsa_cmp_w1': nsa_cmp_w1[i], 'nsa_cmp_w2': nsa_cmp_w2[i],
            'w_branch': w_branch[i], 'w_out': w_out[i], 'moe_w_gu': moe_w_gu[i], 'moe_w_down': moe_w_down[i],
            'w_router': w_router, 'router_bias': router_bias,
        }
        xp, st_p = decoder_layer(xp, c_prompt, pos_p, None, p)
        past = (gather_pages(cache_mla_ckv[i], page_table), gather_pages(cache_mla_kpe[i], page_table),
                gather_pages(cache_nsa_cmp[i], page_table), gather_pages(cache_nsa_slc[i], page_table),
                state_nsa_win[i], state_ssm[i], state_conv[i])
        xs, st_s = decoder_layer(xs, c_sample, pos_s, past, p)
        sp.append(st_p)
        ss.append(st_s)

    def st(outs, j):
        return jnp.stack([o[j] for o in outs], axis=0)

    return (xp, xs, st(sp, 0), st(ss, 0), st(sp, 1), st(ss, 1), st(sp, 2), st(ss, 2), st(sp, 3), st(ss, 3),
            st(sp, 4), st(ss, 4), st(sp, 5), st(ss, 5), st(sp, 6), st(ss, 6))
```

```python
import functools
import math

import jax
import jax.numpy as jnp
from jax import lax
from jax.experimental import pallas as pl
from jax.experimental.pallas import tpu as pltpu
import numpy as np

D_MODEL = 4096
DEPTH = 2
EPS = 1e-6
QBLOCK = 128
MIX_W = D_MODEL // 2
N_BRANCH = 3
SSM_HEADDIM = 64
SSM_HEADS = MIX_W // SSM_HEADDIM
SSM_INNER = SSM_HEADS * SSM_HEADDIM
SSM_GROUPS = 4
SSM_DSTATE = 128
SSM_CONV = 4
SSM_CHUNK = 128
SSM_CONV_DIM = SSM_INNER + 2 * SSM_GROUPS * SSM_DSTATE
MLA_HEADS = 16
MLA_NOPE = 128
MLA_ROPE = 64
MLA_QK = MLA_NOPE + MLA_ROPE
MLA_V = MIX_W // MLA_HEADS
MLA_Q_LORA = 768
MLA_KV_LORA = 256
ROPE_THETA = 10000.0
NSA_D = 64
NSA_HEADS = MIX_W // NSA_D
CMP_STRIDE = 16
CMP_BLOCK = 2 * CMP_STRIDE
CMP_HIDDEN = 256
SLC_BLOCK = 64
SLC_TOPN = 16
N_LOCAL = 2
FORCE_BONUS = 1e6
WINDOW = 512
N_EXPERTS = 16
N_GROUPS = 4
EXPERTS_PER_GROUP = N_EXPERTS // N_GROUPS
GROUP_SCORE_TOPK = 2
TOP_K = 2
D_FF = 768

LANE = 128
VMEM_LIMIT = 56 * 1024 * 1024

BF16 = jnp.bfloat16
F32 = jnp.float32

_SEC = dict(z=SSM_INNER, xbc=SSM_CONV_DIM, dt=SSM_HEADS, cq=MLA_Q_LORA, ckv=MLA_KV_LORA, kpe=MLA_ROPE,
            nq=NSA_HEADS * NSA_D, nkv=6 * NSA_D, ngate=3 * NSA_HEADS, bgate=N_BRANCH * D_MODEL)
_SRC_ORDER = ('z', 'xbc', 'dt', 'cq', 'ckv', 'kpe', 'nq', 'nkv', 'ngate', 'bgate')
_DST_ORDER = ('bgate', 'z', 'xbc', 'cq', 'ckv', 'nq', 'nkv', 'dt', 'kpe', 'ngate')
_SRC_OFF = {}
_o = 0
for _n in _SRC_ORDER:
    _SRC_OFF[_n] = _o
    _o += _SEC[_n]
IN_COLS = _o
_DST_OFF = {}
_o = 0
for _n in _DST_ORDER:
    _DST_OFF[_n] = _o
    _o += _SEC[_n]
IN_TN = 512
IN_COLS_PAD = -(-_o // IN_TN) * IN_TN


def _cparams(sem):
    return pltpu.CompilerParams(dimension_semantics=sem, vmem_limit_bytes=VMEM_LIMIT)


def _mm_body(x_ref, w_ref, o_ref):
    o_ref[...] = jnp.dot(x_ref[...].astype(BF16), w_ref[...].astype(BF16), preferred_element_type=F32)


def _mm_bias_body(x_ref, w_ref, b_ref, o_ref):
    o_ref[...] = jnp.dot(x_ref[...].astype(BF16), w_ref[...].astype(BF16),
                         preferred_element_type=F32) + b_ref[...]


def matmul(x, w, tm, tn, bias=None, name='mm'):
    M, K = x.shape
    _, N = w.shape
    assert M % tm == 0 and N % tn == 0, (M, N, tm, tn)
    in_specs = [pl.BlockSpec((tm, K), lambda i, j: (i, 0)), pl.BlockSpec((K, tn), lambda i, j: (0, j))]
    args = [x, w]
    body = _mm_body
    if bias is not None:
        in_specs.append(pl.BlockSpec((1, tn), lambda i, j: (0, j)))
        args.append(bias.reshape(1, N))
        body = _mm_bias_body
    return pl.pallas_call(
        body, grid=(M // tm, N // tn), in_specs=in_specs,
        out_specs=pl.BlockSpec((tm, tn), lambda i, j: (i, j)),
        out_shape=jax.ShapeDtypeStruct((M, N), F32),
        compiler_params=_cparams(('parallel', 'parallel')), name=name)(*args)


def _row_body(*refs, has_delta, has_norm, has_router):
    it = iter(refs)
    x_ref = next(it)
    if has_delta:
        d_ref, ga_ref = next(it), next(it)
    if has_norm:
        g_ref, sc_ref, sh_ref = next(it), next(it), next(it)
    if has_router:
        wr_ref = next(it)
    x = x_ref[...]
    if has_delta:
        x = x + ga_ref[...] * d_ref[...]
        xo_ref = next(it)
        xo_ref[...] = x
    if has_norm:
        hn_ref = next(it)
        y = x * lax.rsqrt(jnp.mean(x * x, axis=-1, keepdims=True) + EPS)
        hn = y * g_ref[...] * (1.0 + sc_ref[...]) + sh_ref[...]
        hn_ref[...] = hn.astype(BF16)
        if has_router:
            lg_ref = next(it)
            lg_ref[...] = jnp.dot(hn, wr_ref[...], preferred_element_type=F32,
                                  precision=lax.Precision.HIGHEST)


def rowwise(x, delta, ga, g, sc, sh, w_router, rows_per_mod, tr, name):
    R, D = x.shape
    has_delta = delta is not None
    has_norm = g is not None
    has_router = w_router is not None
    assert R % tr == 0
    row_spec = pl.BlockSpec((tr, D), lambda i: (i, 0))
    if rows_per_mod == 1:
        mod_spec = row_spec
        mod_arr = lambda a: a
    else:
        assert rows_per_mod % tr == 0
        k = rows_per_mod // tr
        mod_spec = pl.BlockSpec((None, 1, D), lambda i: (i // k, 0, 0))
        mod_arr = lambda a: a.reshape(a.shape[0], 1, D)
    args, in_specs = [x], [row_spec]
    if has_delta:
        args += [delta, mod_arr(ga)]
        in_specs += [row_spec, mod_spec]
    if has_norm:
        args += [g.reshape(1, D), mod_arr(sc), mod_arr(sh)]
        in_specs += [pl.BlockSpec((1, D), lambda i: (0, 0)), mod_spec, mod_spec]
    if has_router:
        args.append(w_router)
        in_specs.append(pl.BlockSpec((D, N_EXPERTS), lambda i: (0, 0)))
    out_shape, out_specs = [], []
    if has_delta:
        out_shape.append(jax.ShapeDtypeStruct((R, D), F32))
        out_specs.append(row_spec)
    if has_norm:
        out_shape.append(jax.ShapeDtypeStruct((R, D), BF16))
        out_specs.append(row_spec)
        if has_router:
            out_shape.append(jax.ShapeDtypeStruct((R, N_EXPERTS), F32))
            out_specs.append(pl.BlockSpec((tr, N_EXPERTS), lambda i: (i, 0)))
    body = functools.partial(_row_body, has_delta=has_delta, has_norm=has_norm, has_router=has_router)
    return pl.pallas_call(body, grid=(R // tr,), in_specs=in_specs, out_specs=out_specs, out_shape=out_shape,
                          compiler_params=_cparams(('parallel',)), name=name)(*args)


def _bmix_body(br_ref, w_ref, g_ref, o_ref):
    m = pl.program_id(2)
    p = jnp.dot(br_ref[...], w_ref[...], preferred_element_type=F32)
    t = jax.nn.sigmoid(g_ref[...]) * p

    @pl.when(m == 0)
    def _():
        o_ref[...] = t

    @pl.when(m > 0)
    def _():
        o_ref[...] += t


def branch_mix(branches, w_branch, proj, tm, tn):
    _, T, K = branches.shape
    g0 = _DST_OFF['bgate'] // tn
    assert _DST_OFF['bgate'] % tn == 0 and T % tm == 0
    nj = D_MODEL // tn
    return pl.pallas_call(
        _bmix_body, grid=(T // tm, nj, N_BRANCH),
        in_specs=[pl.BlockSpec((None, tm, K), lambda i, j, m: (m, i, 0)),
                  pl.BlockSpec((None, K, tn), lambda i, j, m: (m, 0, j)),
                  pl.BlockSpec((tm, tn), lambda i, j, m: (i, g0 + m * nj + j))],
        out_specs=pl.BlockSpec((tm, tn), lambda i, j, m: (i, j)),
        out_shape=jax.ShapeDtypeStruct((T, D_MODEL), F32),
        compiler_params=_cparams(('parallel', 'parallel', 'arbitrary')), name='branch_mix')(branches, w_branch, proj)


MOE_TM = 512
MOE_FC = 384
MOE_NF = D_FF // MOE_FC


def _moe_body(te_ref, tv_ref, x_ref, wg_ref, wu_ref, wd_ref, rw_ref, o_ref):
    i = pl.program_id(0)
    f = pl.program_id(1)

    @pl.when(tv_ref[i] == 0)
    def _():
        o_ref[...] = jnp.zeros_like(o_ref)

    @pl.when(tv_ref[i] != 0)
    def _():
        x = x_ref[...]
        g = jnp.dot(x, wg_ref[...], preferred_element_type=F32)
        u = jnp.dot(x, wu_ref[...], preferred_element_type=F32)
        a = (jax.nn.silu(g) * u * rw_ref[...]).astype(BF16)
        t = jnp.dot(a, wd_ref[...], preferred_element_type=F32)

        @pl.when(f == 0)
        def _():
            o_ref[...] = t

        @pl.when(f > 0)
        def _():
            o_ref[...] += t


def moe_grouped(xs, row_w, tile_e, tile_v, w_gu, w_down):
    P, D = xs.shape
    nt = P // MOE_TM
    grid_spec = pltpu.PrefetchScalarGridSpec(
        num_scalar_prefetch=2, grid=(nt, MOE_NF),
        in_specs=[pl.BlockSpec((MOE_TM, D), lambda i, f, te, tv: (i, 0)),
                  pl.BlockSpec((None, D, MOE_FC), lambda i, f, te, tv: (te[i], 0, f)),
                  pl.BlockSpec((None, D, MOE_FC), lambda i, f, te, tv: (te[i], 0, f + MOE_NF)),
                  pl.BlockSpec((None, MOE_FC, D), lambda i, f, te, tv: (te[i], f, 0)),
                  pl.BlockSpec((MOE_TM, 1), lambda i, f, te, tv: (i, 0))],
        out_specs=pl.BlockSpec((MOE_TM, D), lambda i, f, te, tv: (i, 0)))
    return pl.pallas_call(
        _moe_body, grid_spec=grid_spec, out_shape=jax.ShapeDtypeStruct((P, D), F32),
        compiler_params=_cparams(('parallel', 'arbitrary')), name='moe_grouped')(
            tile_e, tile_v, xs, w_gu, w_gu, w_down, row_w)


def moe_route(logits, router_bias):
    scores = jax.nn.sigmoid(logits)
    biased = scores + router_bias
    grp = jnp.sum(lax.top_k(biased.reshape(-1, N_GROUPS, EXPERTS_PER_GROUP), GROUP_SCORE_TOPK)[0], axis=-1)
    best = jnp.argmax(grp, axis=-1)
    in_grp = (jnp.arange(N_EXPERTS) // EXPERTS_PER_GROUP)[None, :] == best[:, None]
    _, idx = lax.top_k(jnp.where(in_grp, biased, -jnp.inf), TOP_K)
    w = jnp.take_along_axis(scores, idx, axis=-1)
    w = w / jnp.sum(w, axis=-1, keepdims=True)
    return idx.astype(jnp.int32), w


def moe_ffn(h2, logits, router_bias, w_gu, w_down):
    T, D = h2.shape
    idx, w = moe_route(logits, router_bias)
    A = T * TOP_K
    flat_e = idx.reshape(A)
    order = jnp.argsort(flat_e, stable=True).astype(jnp.int32)
    sorted_e = flat_e[order]
    counts = jnp.zeros((N_EXPERTS,), jnp.int32).at[flat_e].add(1)
    padded = (counts + MOE_TM - 1) // MOE_TM * MOE_TM
    start_u = jnp.cumsum(counts) - counts
    end_p = jnp.cumsum(padded)
    start_p = end_p - padded
    dest_sorted = start_p[sorted_e] + (jnp.arange(A, dtype=jnp.int32) - start_u[sorted_e])
    P = -(-(A + N_EXPERTS * (MOE_TM - 1)) // MOE_TM) * MOE_TM
    row_src = jnp.zeros((P,), jnp.int32).at[dest_sorted].set(order // TOP_K)
    row_w = jnp.zeros((P,), F32).at[dest_sorted].set(w.reshape(A)[order])
    dest = jnp.zeros((A,), jnp.int32).at[order].set(dest_sorted).reshape(T, TOP_K)
    nt = P // MOE_TM
    tile_start = jnp.arange(nt, dtype=jnp.int32) * MOE_TM
    tile_e = jnp.minimum(jnp.searchsorted(end_p, tile_start, side='right'), N_EXPERTS - 1).astype(jnp.int32)
    tile_v = (tile_start < end_p[-1]).astype(jnp.int32)
    last_e = tile_e[jnp.maximum(jnp.sum(tile_v) - 1, 0)]
    tile_e = jnp.where(tile_v != 0, tile_e, last_e)
    xs = jnp.take(h2, row_src, axis=0)
    out = moe_grouped(xs, row_w.reshape(P, 1), tile_e, tile_v, w_gu, w_down)
    return jnp.take(out, dest[:, 0], axis=0) + jnp.take(out, dest[:, 1], axis=0)


def rmsnorm(x, g):
    xf = x.astype(F32)
    y = xf * lax.rsqrt(jnp.mean(xf * xf, axis=-1, keepdims=True) + EPS)
    return y.astype(x.dtype) * g


def rope(x, pos):
    half = x.shape[-1] // 2
    freqs = ROPE_THETA ** (-jnp.arange(half, dtype=F32) / half)
    ang = pos.astype(F32)[:, None] * freqs[None, :]
    cos = jnp.cos(ang)[None, :, None, :].astype(x.dtype)
    sin = jnp.sin(ang)[None, :, None, :].astype(x.dtype)
    x1, x2 = x[..., :half], x[..., half:]
    return jnp.concatenate([x1 * cos - x2 * sin, x1 * sin + x2 * cos], axis=-1)


def masked_softmax(s, mask):
    s = jnp.where(mask, s.astype(F32), -jnp.inf)
    m = jnp.max(s, axis=-1, keepdims=True)
    m = jnp.where(jnp.isfinite(m), m, 0.0)
    p = jnp.exp(s - m)
    return p / jnp.maximum(jnp.sum(p, axis=-1, keepdims=True), 1e-30)


def attend_shared_kv(q, k, v, mask, scale):
    s = jnp.einsum('...qhd,...kd->...hqk', q, k) * scale
    p = masked_softmax(s, mask)
    return jnp.einsum('...hqk,...kd->...qhd', p.astype(v.dtype), v), p


def query_block(n):
    return QBLOCK if n % QBLOCK == 0 else n


def split_query_blocks(t, qb):
    b, n = t.shape[:2]
    return jnp.moveaxis(t.reshape((b, n // qb, qb) + t.shape[2:]), 1, 0)


def merge_query_blocks(t):
    t = jnp.moveaxis(t, 0, 1)
    return t.reshape((t.shape[0], t.shape[1] * t.shape[2]) + t.shape[3:])


def gather_pages(pool, page_table):
    g = pool[page_table]
    return g.reshape((g.shape[0], g.shape[1] * g.shape[2]) + g.shape[3:])


def causal_conv(u, buf, w, b):
    K = w.shape[0]
    L = u.shape[1]
    up = jnp.concatenate([buf.astype(u.dtype), u], axis=1)
    out = b + sum(w[k] * up[:, k:k + L] for k in range(K))
    return out, up[:, up.shape[1] - (K - 1):]


def ssd_scan(x, dt, A, Bm, Cm, h0):
    b, l = x.shape[:2]
    T = SSM_CHUNK if l % SSM_CHUNK == 0 else l
    nc = l // T
    chunk = lambda t: t.reshape((b, nc, T) + t.shape[2:])
    xdt = chunk(x * dt[..., None])
    Bc, Cc = chunk(Bm), chunk(Cm)
    a_cum = jnp.cumsum(chunk(dt * A), axis=2)
    ac = jnp.moveaxis(a_cum, 2, -1)
    causal = jnp.tril(jnp.ones((T, T), dtype=bool))
    decay = jnp.exp(jnp.where(causal, ac[..., :, None] - ac[..., None, :], -jnp.inf))
    y_diag = jnp.einsum('bcign,bcjgn,bcgrij,bcjgrp->bcigrp', Cc, Bc, decay, xdt)
    decay_end = jnp.exp(a_cum[:, :, -1:] - a_cum)
    chunk_states = jnp.einsum('bcjgn,bcjgr,bcjgrp->bcgrpn', Bc, decay_end, xdt)
    chunk_decay = jnp.exp(a_cum[:, :, -1])

    def step(h, inp):
        st, dec = inp
        return dec[..., None, None] * h + st, h

    h_final, h_prev = lax.scan(step, h0.astype(chunk_states.dtype),
                               (jnp.moveaxis(chunk_states, 1, 0), jnp.moveaxis(chunk_decay, 1, 0)))
    h_prev = jnp.moveaxis(h_prev, 0, 1)
    y_off = jnp.einsum('bcign,bcgrpn,bcigr->bcigrp', Cc, h_prev, jnp.exp(a_cum))
    return (y_diag + y_off).reshape(x.shape), h_final


def ssm_mixer(z, xbc, dt_raw, conv_buf, h0, conv_w, conv_b, dt_bias, a_log, d_skip, g_norm):
    b, l = z.shape[:2]
    R = SSM_HEADS // SSM_GROUPS
    xbc, new_buf = causal_conv(xbc, conv_buf, conv_w, conv_b)
    xbc = jax.nn.silu(xbc)
    xs, Bm, Cm = jnp.split(xbc, [SSM_INNER, SSM_INNER + SSM_GROUPS * SSM_DSTATE], axis=-1)
    xs = xs.reshape(b, l, SSM_GROUPS, R, SSM_HEADDIM)
    Bm = Bm.reshape(b, l, SSM_GROUPS, SSM_DSTATE)
    Cm = Cm.reshape(b, l, SSM_GROUPS, SSM_DSTATE)
    dt = jax.nn.softplus((dt_raw + dt_bias).astype(F32)).reshape(b, l, SSM_GROUPS, R)
    A = -jnp.exp(a_log.astype(F32)).reshape(SSM_GROUPS, R)
    y, h_final = ssd_scan(xs, dt, A, Bm, Cm, h0.reshape(b, SSM_GROUPS, R, SSM_HEADDIM, SSM_DSTATE))
    y = y + d_skip.reshape(SSM_GROUPS, R)[:, :, None] * xs
    y = rmsnorm(y.reshape(b, l, SSM_INNER).astype(z.dtype) * jax.nn.silu(z), g_norm)
    return y, h_final.reshape(b, SSM_HEADS, SSM_HEADDIM, SSM_DSTATE), new_buf


def mla_attend(q_lat, q_pe, ckv, kpe, q_pos, k_pos):
    qb = query_block(q_lat.shape[1])
    scale = MLA_QK ** -0.5

    def one(args):
        ql, qp, qpos = args
        s = jnp.einsum('bqhk,bsk->bhqs', ql, ckv) + jnp.einsum('bqhr,bsr->bhqs', qp, kpe)
        p = masked_softmax(s * scale, k_pos[None, :] <= qpos[:, None])
        return jnp.einsum('bhqs,bsk->bqhk', p.astype(ckv.dtype), ckv)

    out = lax.map(one, (split_query_blocks(q_lat, qb), split_query_blocks(q_pe, qb), q_pos.reshape(-1, qb)))
    return merge_query_blocks(out)


def mla_mixer(cq, ckv, kpe, pos, past_ckv, past_kpe, g_q_a, w_uq, g_q, g_kv_a, g_kpe, w_uk, w_uv):
    b, l = cq.shape[:2]
    q = (rmsnorm(cq, g_q_a) @ w_uq).reshape(b, l, MLA_HEADS, MLA_QK)
    q = rmsnorm(q, g_q)
    q_nope, q_pe = q[..., :MLA_NOPE], rope(q[..., MLA_NOPE:], pos)
    ckv = rmsnorm(ckv, g_kv_a)
    kpe = rope(rmsnorm(kpe, g_kpe)[:, :, None, :], pos)[:, :, 0, :]
    if past_ckv is None:
        keys_ckv, keys_kpe, k_pos = ckv, kpe, pos
    else:
        keys_ckv = jnp.concatenate([past_ckv.astype(ckv.dtype), ckv], axis=1)
        keys_kpe = jnp.concatenate([past_kpe.astype(kpe.dtype), kpe], axis=1)
        k_pos = jnp.arange(keys_ckv.shape[1], dtype=jnp.int32)
    q_lat = jnp.einsum('blhd,khd->blhk', q_nope, w_uk)
    o_lat = mla_attend(q_lat, q_pe, keys_ckv, keys_kpe, pos, k_pos)
    o = jnp.einsum('blhk,khv->blhv', o_lat, w_uv).reshape(b, l, MLA_HEADS * MLA_V)
    return o, ckv, kpe


def compress(rows, pos_emb, w1, w2):
    b, T = rows.shape[:2]
    nc = T // CMP_STRIDE
    chunks = rows[:, :nc * CMP_STRIDE].reshape(b, nc, CMP_STRIDE, 2, NSA_D)
    w1a, w1b = w1[:, :CMP_STRIDE], w1[:, CMP_STRIDE:]
    first = jnp.einsum('bnskd,ksde->bnke', chunks, w1a)[:, :-1]
    second = jnp.einsum('bnskd,ksde->bnke', chunks, w1b)[:, 1:]
    pos_term = jnp.einsum('skd,ksde->ke', pos_emb, w1)
    h = jax.nn.gelu(first + second + pos_term)
    out = jnp.einsum('bnke,ked->bnkd', h, w2)
    return out[:, :, 0], out[:, :, 1]


def slc_importance(p_cmp, n_slc):
    b, lq, n_cmp = p_cmp.shape
    R = SLC_BLOCK // CMP_STRIDE
    pp = jnp.pad(p_cmp, ((0, 0), (0, 0), (1, R * (n_slc + 1) - 1 - n_cmp))).reshape(b, lq, n_slc + 1, R)
    return pp[..., :-1, 0] + 2.0 * jnp.sum(pp[..., :-1, 1:], axis=-1) + pp[..., 1:, 0]


def select_blocks(imp, pos, n_slc):
    j = jnp.arange(n_slc, dtype=jnp.int32)[None, :]
    cur = (pos // SLC_BLOCK)[:, None]
    valid = j <= cur
    forced = (j == 0) | (j >= cur - (N_LOCAL - 1))
    score = jnp.where(valid, imp + jnp.where(forced, FORCE_BONUS, 0.0), -jnp.inf)
    vals, idx = lax.top_k(score, min(SLC_TOPN, n_slc))
    return idx, jnp.isfinite(vals)


def slc_attend(q, slc_blocks, pos, idx, sel_valid, scale):
    b, lq = q.shape[:2]
    qb = query_block(lq)
    n_sel = idx.shape[-1]
    offs = jnp.arange(SLC_BLOCK, dtype=jnp.int32)

    def one(args):
        qq, pp, ii, vv = args
        g = jax.vmap(lambda blk, ix: blk[ix])(slc_blocks, ii)
        g = g.reshape(b, qb, n_sel * SLC_BLOCK, 2, NSA_D)
        kpos = (ii[..., None] * SLC_BLOCK + offs).reshape(b, qb, n_sel * SLC_BLOCK)
        mask = jnp.repeat(vv, SLC_BLOCK, axis=-1) & (kpos <= pp[None, :, None])
        o, _ = attend_shared_kv(qq[:, :, None], g[:, :, :, 0], g[:, :, :, 1], mask[:, :, None, None, :], scale)
        return o[:, :, 0]

    out = lax.map(one, (split_query_blocks(q, qb), pos.reshape(-1, qb),
                        split_query_blocks(idx, qb), split_query_blocks(sel_valid, qb)))
    return merge_query_blocks(out)


def window_attend_banded(q, k, v, pos, scale):
    b, l = q.shape[:2]
    qb = query_block(l)
    nb = l // qb
    nw = -(-WINDOW // qb)
    pad = lambda t: jnp.pad(t, ((0, 0), (nw * qb, 0), (0, 0))).reshape(b, nw + nb, qb, NSA_D)
    band = jnp.arange(nb)[:, None] + jnp.arange(nw + 1)[None, :]
    kb = pad(k)[:, band].reshape(b, nb, (nw + 1) * qb, NSA_D)
    vb = pad(v)[:, band].reshape(b, nb, (nw + 1) * qb, NSA_D)
    kpos = (band[..., None] * qb + jnp.arange(qb)).reshape(nb, -1) - nw * qb
    dist = pos.reshape(nb, qb)[:, :, None] - kpos[:, None, :]
    mask = (dist >= 0) & (dist < WINDOW) & (kpos[:, None, :] >= 0)
    o, _ = attend_shared_kv(q.reshape(b, nb, qb, NSA_HEADS, NSA_D), kb, vb, mask[None, :, None], scale)
    return o.reshape(b, l, NSA_HEADS, NSA_D)


def nsa_mixer(q, kv, gates, pos, past_cmp, past_slc, win_buf, g_q, g_k, cmp_pos, cmp_w1, cmp_w2):
    b, l = q.shape[:2]
    scale = NSA_D ** -0.5
    q = rmsnorm(q.reshape(b, l, NSA_HEADS, NSA_D), g_q)
    kv = kv.reshape(b, l, 3, 2, NSA_D)
    cmp_rows = kv[:, :, 0]
    slc_rows = jnp.stack([rmsnorm(kv[:, :, 1, 0], g_k[1]), kv[:, :, 1, 1]], axis=2)
    win_rows = jnp.stack([rmsnorm(kv[:, :, 2, 0], g_k[2]), kv[:, :, 2, 1]], axis=2)
    if past_cmp is None:
        cmp_all, slc_all = cmp_rows, slc_rows
    else:
        cmp_all = jnp.concatenate([past_cmp.astype(cmp_rows.dtype), cmp_rows], axis=1)
        slc_all = jnp.concatenate([past_slc.astype(slc_rows.dtype), slc_rows], axis=1)
    kc, vc = compress(cmp_all, cmp_pos, cmp_w1, cmp_w2)
    kc = rmsnorm(kc, g_k[0])
    cmp_end = jnp.arange(kc.shape[1], dtype=jnp.int32) * CMP_STRIDE + CMP_BLOCK - 1
    o_cmp, p_cmp = attend_shared_kv(q, kc, vc, (cmp_end[None, :] <= pos[:, None])[None, None], scale)
    T = slc_all.shape[1]
    n_slc = -(-T // SLC_BLOCK)
    imp = slc_importance(jnp.sum(p_cmp, axis=1), n_slc)
    idx, sel_valid = select_blocks(imp, pos, n_slc)
    slc_pad = jnp.pad(slc_all, ((0, 0), (0, n_slc * SLC_BLOCK - T), (0, 0), (0, 0)))
    o_slc = slc_attend(q, slc_pad.reshape(b, n_slc, SLC_BLOCK, 2, NSA_D), pos, idx, sel_valid, scale)
    if win_buf is None:
        o_win = window_attend_banded(q, win_rows[:, :, 0], win_rows[:, :, 1], pos, scale)
        new_win = win_rows[:, max(l - WINDOW, 0):]
    else:
        wb = win_buf.shape[1]
        rows = jnp.concatenate([win_buf.astype(win_rows.dtype), win_rows], axis=1)
        kpos = jnp.arange(rows.shape[1], dtype=jnp.int32) + (pos[0] - wb)
        dist = pos[:, None] - kpos[None, :]
        mask = (dist >= 0) & (dist < WINDOW)
        o_win, _ = attend_shared_kv(q, rows[:, :, 0], rows[:, :, 1], mask[None, None], scale)
        new_win = rows[:, rows.shape[1] - wb:]
    g = jax.nn.sigmoid(gates.reshape(b, l, 3, NSA_HEADS).astype(F32))[..., None]
    o = g[:, :, 0] * o_cmp + g[:, :, 1] * o_slc + g[:, :, 2] * o_win
    return o.reshape(b, l, NSA_HEADS * NSA_D).astype(q.dtype), cmp_rows, slc_rows, new_win


def mixers(proj, b, l, pos, past, p):
    def sec(name):
        o = _DST_OFF[name]
        return proj[:, o:o + _SEC[name]].reshape(b, l, _SEC[name])

    if past is None:
        past_ckv = past_kpe = past_cmp = past_slc = win_buf = None
        h0 = jnp.zeros((b, SSM_HEADS, SSM_HEADDIM, SSM_DSTATE), F32)
        conv_buf = jnp.zeros((b, SSM_CONV - 1, SSM_CONV_DIM), F32)
    else:
        past_ckv, past_kpe, past_cmp, past_slc, win_buf, h0, conv_buf = past
    o_ssm, h_new, conv_new = ssm_mixer(sec('z'), sec('xbc'), sec('dt'), conv_buf, h0, p['ssm_conv_w'],
                                       p['ssm_conv_b'], p['ssm_dt_bias'], p['ssm_a_log'], p['ssm_d'],
                                       p['ssm_g_norm'])
    o_mla, ckv_new, kpe_new = mla_mixer(sec('cq'), sec('ckv'), sec('kpe'), pos, past_ckv, past_kpe,
                                        p['mla_g_q_a'], p['mla_w_uq'], p['mla_g_q'], p['mla_g_kv_a'],
                                        p['mla_g_kpe'], p['mla_w_uk'], p['mla_w_uv'])
    o_nsa, cmp_new, slc_new, win_new = nsa_mixer(sec('nq'), sec('nkv'), sec('ngate'), pos, past_cmp, past_slc,
                                                 win_buf, p['nsa_g_q'], p['nsa_g_k'], p['nsa_cmp_pos'],
                                                 p['nsa_cmp_w1'], p['nsa_cmp_w2'])
    br = jnp.stack([o_ssm.reshape(b * l, MIX_W), o_mla.reshape(b * l, MIX_W), o_nsa.reshape(b * l, MIX_W)], axis=0)
    return br, (ckv_new, kpe_new, cmp_new, slc_new, win_new, h_new, conv_new)


def _regroup_w_in(w):
    parts = [w[:, _SRC_OFF[n]:_SRC_OFF[n] + _SEC[n]] for n in _DST_ORDER]
    wr = jnp.concatenate(parts, axis=1).astype(BF16)
    return jnp.pad(wr, ((0, 0), (0, IN_COLS_PAD - wr.shape[1])))


def kernel(x_prompt, x_sample, cache_mla_ckv, cache_mla_kpe, cache_nsa_cmp, cache_nsa_slc, state_nsa_win, state_ssm, state_conv, page_table, c_prompt, c_sample, w_ada, b_ada, g_norm_mix, g_norm_ffn, w_in, ssm_conv_w, ssm_conv_b, ssm_dt_bias, ssm_a_log, ssm_d, ssm_g_norm, mla_g_q_a, mla_w_uq, mla_g_q, mla_g_kv_a, mla_g_kpe, mla_w_uk, mla_w_uv, nsa_g_q, nsa_g_k, nsa_cmp_pos, nsa_cmp_w1, nsa_cmp_w2, w_branch, w_out, moe_w_gu, moe_w_down, w_router, router_bias):
    B, L, D = x_prompt.shape
    S = x_sample.shape[0]
    TP = B * L
    T = TP + S
    past_len = page_table.shape[1] * cache_mla_ckv.shape[2]
    pos_p = jnp.arange(L, dtype=jnp.int32)
    pos_s = past_len + jnp.arange(x_sample.shape[1], dtype=jnp.int32)
    TR = 256
    TM = 640
    assert T % TM == 0 and L % TR == 0 and S % 8 == 0

    nb = B + S
    nb_pad = -(-nb // 8) * 8
    c_all = jnp.pad(jax.nn.silu(jnp.concatenate([c_prompt, c_sample], axis=0)), ((0, nb_pad - nb), (0, 0)))
    mods = [matmul(c_all, w_ada[i], nb_pad, 512, bias=b_ada[i], name='ada') for i in range(DEPTH)]

    def mod_parts(i):
        m = mods[i].reshape(nb_pad, 6, D)
        return [(m[:B, k], m[B:nb, k]) for k in range(6)]

    def rows2(xp, xs, dp, ds, ga, g, sc, sh, wr, name):
        op = rowwise(xp, dp, None if ga is None else ga[0], g, None if sc is None else sc[0],
                     None if sh is None else sh[0], wr, L, TR, name + '_p')
        os_ = rowwise(xs, ds, None if ga is None else ga[1], g, None if sc is None else sc[1],
                      None if sh is None else sh[1], wr, 1, S, name + '_s')
        return op, os_

    xp = x_prompt.reshape(TP, D)
    xs = x_sample.reshape(S, D)
    sh1, sc1, ga1, sh2, sc2, ga2 = mod_parts(0)
    (hn_p,), (hn_s,) = rows2(xp, xs, None, None, None, g_norm_mix[0], sc1, sh1, None, 'norm0')
    sp, ss = [], []
    for i in range(DEPTH):
        p = {
            'ssm_conv_w': ssm_conv_w[i], 'ssm_conv_b': ssm_conv_b[i],
            'ssm_dt_bias': ssm_dt_bias[i], 'ssm_a_log': ssm_a_log[i], 'ssm_d': ssm_d[i],
            'ssm_g_norm': ssm_g_norm[i], 'mla_g_q_a': mla_g_q_a[i], 'mla_w_uq': mla_w_uq[i],
            'mla_g_q': mla_g_q[i], 'mla_g_kv_a': mla_g_kv_a[i], 'mla_g_kpe': mla_g_kpe[i],
            'mla_w_uk': mla_w_uk[i], 'mla_w_uv': mla_w_uv[i], 'nsa_g_q': nsa_g_q[i], 'nsa_g_k': nsa_g_k[i],
            'nsa_cmp_pos': nsa_cmp_pos[i], 'nsa_cmp_w1': nsa_cmp_w1[i], 'nsa_cmp_w2': nsa_cmp_w2[i],
        }
        hn = jnp.concatenate([hn_p, hn_s], axis=0)
        proj = matmul(hn, _regroup_w_in(w_in[i]), TM, IN_TN, name='in_proj')
        br_p, st_p = mixers(proj[:TP], B, L, pos_p, None, p)
        past = (gather_pages(cache_mla_ckv[i], page_table), gather_pages(cache_mla_kpe[i], page_table),
                gather_pages(cache_nsa_cmp[i], page_table), gather_pages(cache_nsa_slc[i], page_table),
                state_nsa_win[i], state_ssm[i], state_conv[i])
        br_s, st_s = mixers(proj[TP:], S, 1, pos_s, past, p)
        sp.append(st_p)
        ss.append(st_s)
        branches = jnp.concatenate([br_p, br_s], axis=1).astype(BF16)
        mixin = branch_mix(branches, w_branch[i].astype(BF16), proj, TM, 512)
        mix = matmul(mixin, w_out[i].astype(BF16), TM, 512, name='out_proj')
        (x1_p, h2_p, lg_p), (x1_s, h2_s, lg_s) = rows2(xp, xs, mix[:TP], mix[TP:], ga1, g_norm_ffn[i], sc2, sh2,
                                                       w_router, 'ffn_norm')
        moe = moe_ffn(jnp.concatenate([h2_p, h2_s], axis=0), jnp.concatenate([lg_p, lg_s], axis=0),
                      router_bias, moe_w_gu[i].astype(BF16), moe_w_down[i].astype(BF16))
        if i + 1 < DEPTH:
            ga2_cur = ga2
            sh1, sc1, ga1, sh2, sc2, ga2 = mod_parts(i + 1)
            (xp, hn_p), (xs, hn_s) = rows2(x1_p, x1_s, moe[:TP], moe[TP:], ga2_cur, g_norm_mix[i + 1], sc1, sh1,
                                           None, 'mix_norm')
        else:
            (xp,), (xs,) = rows2(x1_p, x1_s, moe[:TP], moe[TP:], ga2, None, None, None, None, 'final_add')

    def st(outs, j):
        return jnp.stack([o[j] for o in outs], axis=0)

    return (xp.reshape(B, L, D), xs.reshape(S, 1, D), st(sp, 0), st(ss, 0), st(sp, 1), st(ss, 1), st(sp, 2),
            st(ss, 2), st(sp, 3), st(ss, 3), st(sp, 4), st(ss, 4), st(sp, 5), st(ss, 5), st(sp, 6), st(ss, 6))
```

```python
import functools
import math

import jax
import jax.numpy as jnp
from jax import lax
from jax.experimental import pallas as pl
from jax.experimental.pallas import tpu as pltpu
import numpy as np

D_MODEL = 4096
DEPTH = 2
EPS = 1e-6
QBLOCK = 128
MIX_W = D_MODEL // 2
N_BRANCH = 3
SSM_HEADDIM = 64
SSM_HEADS = MIX_W // SSM_HEADDIM
SSM_INNER = SSM_HEADS * SSM_HEADDIM
SSM_GROUPS = 4
SSM_DSTATE = 128
SSM_CONV = 4
SSM_CHUNK = 128
SSM_CONV_DIM = SSM_INNER + 2 * SSM_GROUPS * SSM_DSTATE
MLA_HEADS = 16
MLA_NOPE = 128
MLA_ROPE = 64
MLA_QK = MLA_NOPE + MLA_ROPE
MLA_V = MIX_W // MLA_HEADS
MLA_Q_LORA = 768
MLA_KV_LORA = 256
ROPE_THETA = 10000.0
NSA_D = 64
NSA_HEADS = MIX_W // NSA_D
CMP_STRIDE = 16
CMP_BLOCK = 2 * CMP_STRIDE
CMP_HIDDEN = 256
SLC_BLOCK = 64
SLC_TOPN = 16
N_LOCAL = 2
FORCE_BONUS = 1e6
WINDOW = 512
N_EXPERTS = 16
N_GROUPS = 4
EXPERTS_PER_GROUP = N_EXPERTS // N_GROUPS
GROUP_SCORE_TOPK = 2
TOP_K = 2
D_FF = 768

LANE = 128
VMEM_LIMIT = 56 * 1024 * 1024

BF16 = jnp.bfloat16
F32 = jnp.float32

_SEC = dict(z=SSM_INNER, xbc=SSM_CONV_DIM, dt=SSM_HEADS, cq=MLA_Q_LORA, ckv=MLA_KV_LORA, kpe=MLA_ROPE,
            nq=NSA_HEADS * NSA_D, nkv=6 * NSA_D, ngate=3 * NSA_HEADS, bgate=N_BRANCH * D_MODEL)
_SRC_ORDER = ('z', 'xbc', 'dt', 'cq', 'ckv', 'kpe', 'nq', 'nkv', 'ngate', 'bgate')
_DST_ORDER = ('bgate', 'z', 'xbc', 'cq', 'ckv', 'nq', 'nkv', 'dt', 'kpe', 'ngate')
_SRC_OFF = {}
_o = 0
for _n in _SRC_ORDER:
    _SRC_OFF[_n] = _o
    _o += _SEC[_n]
IN_COLS = _o
_DST_OFF = {}
_o = 0
for _n in _DST_ORDER:
    _DST_OFF[_n] = _o
    _o += _SEC[_n]
IN_TN = 512
IN_COLS_PAD = -(-_o // IN_TN) * IN_TN


def _cparams(sem):
    return pltpu.CompilerParams(dimension_semantics=sem, vmem_limit_bytes=VMEM_LIMIT)


def _mm_body(x_ref, w_ref, o_ref):
    o_ref[...] = jnp.dot(x_ref[...].astype(BF16), w_ref[...].astype(BF16), preferred_element_type=F32)


def _mm_bias_body(x_ref, w_ref, b_ref, o_ref):
    o_ref[...] = jnp.dot(x_ref[...].astype(BF16), w_ref[...].astype(BF16),
                         preferred_element_type=F32) + b_ref[...]


def matmul(x, w, tm, tn, bias=None, name='mm'):
    M, K = x.shape
    _, N = w.shape
    assert M % tm == 0 and N % tn == 0, (M, N, tm, tn)
    in_specs = [pl.BlockSpec((tm, K), lambda i, j: (i, 0)), pl.BlockSpec((K, tn), lambda i, j: (0, j))]
    args = [x, w]
    body = _mm_body
    if bias is not None:
        in_specs.append(pl.BlockSpec((1, tn), lambda i, j: (0, j)))
        args.append(bias.reshape(1, N))
        body = _mm_bias_body
    return pl.pallas_call(
        body, grid=(M // tm, N // tn), in_specs=in_specs,
        out_specs=pl.BlockSpec((tm, tn), lambda i, j: (i, j)),
        out_shape=jax.ShapeDtypeStruct((M, N), F32),
        compiler_params=_cparams(('parallel', 'parallel')), name=name)(*args)


def _row_body(*refs, has_delta, has_norm, has_router):
    it = iter(refs)
    x_ref = next(it)
    if has_delta:
        d_ref, ga_ref = next(it), next(it)
    if has_norm:
        g_ref, sc_ref, sh_ref = next(it), next(it), next(it)
    if has_router:
        wr_ref = next(it)
    x = x_ref[...]
    if has_delta:
        x = x + ga_ref[...] * d_ref[...]
        xo_ref = next(it)
        xo_ref[...] = x
    if has_norm:
        hn_ref = next(it)
        y = x * lax.rsqrt(jnp.mean(x * x, axis=-1, keepdims=True) + EPS)
        hn = y * g_ref[...] * (1.0 + sc_ref[...]) + sh_ref[...]
        hn_ref[...] = hn.astype(BF16)
        if has_router:
            lg_ref = next(it)
            lg_ref[...] = jnp.dot(hn, wr_ref[...], preferred_element_type=F32,
                                  precision=lax.Precision.HIGHEST)


def rowwise(x, delta, ga, g, sc, sh, w_router, rows_per_mod, tr, name):
    R, D = x.shape
    has_delta = delta is not None
    has_norm = g is not None
    has_router = w_router is not None
    assert R % tr == 0
    row_spec = pl.BlockSpec((tr, D), lambda i: (i, 0))
    if rows_per_mod == 1:
        mod_spec = row_spec
        mod_arr = lambda a: a
    else:
        assert rows_per_mod % tr == 0
        k = rows_per_mod // tr
        mod_spec = pl.BlockSpec((None, 1, D), lambda i: (i // k, 0, 0))
        mod_arr = lambda a: a.reshape(a.shape[0], 1, D)
    args, in_specs = [x], [row_spec]
    if has_delta:
        args += [delta, mod_arr(ga)]
        in_specs += [row_spec, mod_spec]
    if has_norm:
        args += [g.reshape(1, D), mod_arr(sc), mod_arr(sh)]
        in_specs += [pl.BlockSpec((1, D), lambda i: (0, 0)), mod_spec, mod_spec]
    if has_router:
        args.append(w_router)
        in_specs.append(pl.BlockSpec((D, N_EXPERTS), lambda i: (0, 0)))
    out_shape, out_specs = [], []
    if has_delta:
        out_shape.append(jax.ShapeDtypeStruct((R, D), F32))
        out_specs.append(row_spec)
    if has_norm:
        out_shape.append(jax.ShapeDtypeStruct((R, D), BF16))
        out_specs.append(row_spec)
        if has_router:
            out_shape.append(jax.ShapeDtypeStruct((R, N_EXPERTS), F32))
            out_specs.append(pl.BlockSpec((tr, N_EXPERTS), lambda i: (i, 0)))
    body = functools.partial(_row_body, has_delta=has_delta, has_norm=has_norm, has_router=has_router)
    return pl.pallas_call(body, grid=(R // tr,), in_specs=in_specs, out_specs=out_specs, out_shape=out_shape,
                          compiler_params=_cparams(('parallel',)), name=name)(*args)


def _bmix_body(br_ref, w_ref, g_ref, o_ref):
    m = pl.program_id(2)
    p = jnp.dot(br_ref[...], w_ref[...], preferred_element_type=F32)
    t = jax.nn.sigmoid(g_ref[...]) * p

    @pl.when(m == 0)
    def _():
        o_ref[...] = t

    @pl.when(m > 0)
    def _():
        o_ref[...] += t


def branch_mix(branches, w_branch, proj, tm, tn):
    _, T, K = branches.shape
    g0 = _DST_OFF['bgate'] // tn
    assert _DST_OFF['bgate'] % tn == 0 and T % tm == 0
    nj = D_MODEL // tn
    return pl.pallas_call(
        _bmix_body, grid=(T // tm, nj, N_BRANCH),
        in_specs=[pl.BlockSpec((None, tm, K), lambda i, j, m: (m, i, 0)),
                  pl.BlockSpec((None, K, tn), lambda i, j, m: (m, 0, j)),
                  pl.BlockSpec((tm, tn), lambda i, j, m: (i, g0 + m * nj + j))],
        out_specs=pl.BlockSpec((tm, tn), lambda i, j, m: (i, j)),
        out_shape=jax.ShapeDtypeStruct((T, D_MODEL), F32),
        compiler_params=_cparams(('parallel', 'parallel', 'arbitrary')), name='branch_mix')(branches, w_branch, proj)


MOE_TM = 512
MOE_FC = 384
MOE_NF = D_FF // MOE_FC


def _moe_body(te_ref, tv_ref, x_ref, wg_ref, wu_ref, wd_ref, rw_ref, o_ref):
    i = pl.program_id(0)
    f = pl.program_id(1)

    @pl.when(tv_ref[i] == 0)
    def _():
        o_ref[...] = jnp.zeros_like(o_ref)

    @pl.when(tv_ref[i] != 0)
    def _():
        x = x_ref[...]
        g = jnp.dot(x, wg_ref[...], preferred_element_type=F32)
        u = jnp.dot(x, wu_ref[...], preferred_element_type=F32)
        a = (jax.nn.silu(g) * u * rw_ref[...]).astype(BF16)
        t = jnp.dot(a, wd_ref[...], preferred_element_type=F32)

        @pl.when(f == 0)
        def _():
            o_ref[...] = t

        @pl.when(f > 0)
        def _():
            o_ref[...] += t


def moe_grouped(xs, row_w, tile_e, tile_v, w_gu, w_down):
    P, D = xs.shape
    nt = P // MOE_TM
    grid_spec = pltpu.PrefetchScalarGridSpec(
        num_scalar_prefetch=2, grid=(nt, MOE_NF),
        in_specs=[pl.BlockSpec((MOE_TM, D), lambda i, f, te, tv: (i, 0)),
                  pl.BlockSpec((None, D, MOE_FC), lambda i, f, te, tv: (te[i], 0, f)),
                  pl.BlockSpec((None, D, MOE_FC), lambda i, f, te, tv: (te[i], 0, f + MOE_NF)),
                  pl.BlockSpec((None, MOE_FC, D), lambda i, f, te, tv: (te[i], f, 0)),
                  pl.BlockSpec((MOE_TM, 1), lambda i, f, te, tv: (i, 0))],
        out_specs=pl.BlockSpec((MOE_TM, D), lambda i, f, te, tv: (i, 0)))
    return pl.pallas_call(
        _moe_body, grid_spec=grid_spec, out_shape=jax.ShapeDtypeStruct((P, D), F32),
        compiler_params=_cparams(('parallel', 'arbitrary')), name='moe_grouped')(
            tile_e, tile_v, xs, w_gu, w_gu, w_down, row_w)


def moe_route(logits, router_bias):
    scores = jax.nn.sigmoid(logits)
    biased = scores + router_bias
    grp = jnp.sum(lax.top_k(biased.reshape(-1, N_GROUPS, EXPERTS_PER_GROUP), GROUP_SCORE_TOPK)[0], axis=-1)
    best = jnp.argmax(grp, axis=-1)
    in_grp = (jnp.arange(N_EXPERTS) // EXPERTS_PER_GROUP)[None, :] == best[:, None]
    _, idx = lax.top_k(jnp.where(in_grp, biased, -jnp.inf), TOP_K)
    w = jnp.take_along_axis(scores, idx, axis=-1)
    w = w / jnp.sum(w, axis=-1, keepdims=True)
    return idx.astype(jnp.int32), w


def moe_ffn(h2, logits, router_bias, w_gu, w_down):
    T, D = h2.shape
    idx, w = moe_route(logits, router_bias)
    A = T * TOP_K
    flat_e = idx.reshape(A)
    onehot = (flat_e[:, None] == jnp.arange(N_EXPERTS, dtype=jnp.int32)[None, :]).astype(jnp.int32)
    csum = jnp.cumsum(onehot, axis=0)
    rank = jnp.sum(csum * onehot, axis=1) - 1
    counts = csum[-1]
    padded = (counts + MOE_TM - 1) // MOE_TM * MOE_TM
    end_p = jnp.cumsum(padded)
    start_p = end_p - padded
    dest_flat = jnp.sum(start_p[None, :] * onehot, axis=1) + rank
    P = -(-(A + N_EXPERTS * (MOE_TM - 1)) // MOE_TM) * MOE_TM
    row_src = jnp.zeros((P,), jnp.int32).at[dest_flat].set(jnp.arange(A, dtype=jnp.int32) // TOP_K)
    row_w = jnp.zeros((P,), F32).at[dest_flat].set(w.reshape(A))
    dest = dest_flat.reshape(T, TOP_K)
    nt = P // MOE_TM
    tile_start = jnp.arange(nt, dtype=jnp.int32) * MOE_TM
    tile_e = jnp.minimum(jnp.searchsorted(end_p, tile_start, side='right'), N_EXPERTS - 1).astype(jnp.int32)
    tile_v = (tile_start < end_p[-1]).astype(jnp.int32)
    last_e = tile_e[jnp.maximum(jnp.sum(tile_v) - 1, 0)]
    tile_e = jnp.where(tile_v != 0, tile_e, last_e)
    xs = jnp.take(h2, row_src, axis=0)
    out = moe_grouped(xs, row_w.reshape(P, 1), tile_e, tile_v, w_gu, w_down)
    return jnp.take(out, dest[:, 0], axis=0) + jnp.take(out, dest[:, 1], axis=0)


SLC_TQ = 128


def _slc_dense_body(sel_ref, q_ref, k_ref, v_ref, e_ref, o_ref, *, scale):
    n_heads, tq, _ = q_ref.shape
    n_keys = k_ref.shape[0]
    k = k_ref[...].astype(BF16)
    v = v_ref[...].astype(BF16)
    selx = jnp.dot(sel_ref[...], e_ref[...], preferred_element_type=F32)
    row = pl.program_id(1) * tq + lax.broadcasted_iota(jnp.int32, (tq, n_keys), 0)
    col = lax.broadcasted_iota(jnp.int32, (tq, n_keys), 1)
    bias = jnp.where(col <= row, jnp.where(selx > 0.5, 0.0, -jnp.inf), -jnp.inf)

    def head(h, carry):
        q = (q_ref[h] * scale).astype(BF16)
        s = lax.dot_general(q, k, (((1,), (1,)), ((), ())), preferred_element_type=F32) + bias
        m = jnp.max(s, axis=-1, keepdims=True)
        m = jnp.where(m == -jnp.inf, 0.0, m)
        p = jnp.exp(s - m)
        l = jnp.sum(p, axis=-1, keepdims=True)
        o = jnp.dot(p.astype(BF16), v, preferred_element_type=F32)
        o_ref[h] = o / jnp.maximum(l, 1e-30)
        return carry

    lax.fori_loop(0, n_heads, head, 0)


def slc_attend_dense(q, k, v, sel, scale):
    b, L, H, d = q.shape
    n_blk = sel.shape[-1]
    assert L % SLC_TQ == 0 and n_blk * SLC_BLOCK == L
    expand = (jnp.arange(L, dtype=jnp.int32)[None, :] // SLC_BLOCK
              == jnp.arange(n_blk, dtype=jnp.int32)[:, None]).astype(BF16)
    o = pl.pallas_call(
        functools.partial(_slc_dense_body, scale=scale), grid=(b, L // SLC_TQ),
        in_specs=[pl.BlockSpec((None, SLC_TQ, n_blk), lambda bi, qi: (bi, qi, 0)),
                  pl.BlockSpec((None, H, SLC_TQ, d), lambda bi, qi: (bi, 0, qi, 0)),
                  pl.BlockSpec((None, L, d), lambda bi, qi: (bi, 0, 0)),
                  pl.BlockSpec((None, L, d), lambda bi, qi: (bi, 0, 0)),
                  pl.BlockSpec((n_blk, L), lambda bi, qi: (0, 0))],
        out_specs=pl.BlockSpec((None, H, SLC_TQ, d), lambda bi, qi: (bi, 0, qi, 0)),
        out_shape=jax.ShapeDtypeStruct((b, H, L, d), F32),
        compiler_params=_cparams(('parallel', 'parallel')), name='slc_dense')(
            sel.astype(BF16), jnp.transpose(q, (0, 2, 1, 3)), k, v, expand)
    return jnp.transpose(o, (0, 2, 1, 3))


def rmsnorm(x, g):
    xf = x.astype(F32)
    y = xf * lax.rsqrt(jnp.mean(xf * xf, axis=-1, keepdims=True) + EPS)
    return y.astype(x.dtype) * g


def rope(x, pos):
    half = x.shape[-1] // 2
    freqs = ROPE_THETA ** (-jnp.arange(half, dtype=F32) / half)
    ang = pos.astype(F32)[:, None] * freqs[None, :]
    cos = jnp.cos(ang)[None, :, None, :].astype(x.dtype)
    sin = jnp.sin(ang)[None, :, None, :].astype(x.dtype)
    x1, x2 = x[..., :half], x[..., half:]
    return jnp.concatenate([x1 * cos - x2 * sin, x1 * sin + x2 * cos], axis=-1)


def masked_softmax(s, mask):
    s = jnp.where(mask, s.astype(F32), -jnp.inf)
    m = jnp.max(s, axis=-1, keepdims=True)
    m = jnp.where(jnp.isfinite(m), m, 0.0)
    p = jnp.exp(s - m)
    return p / jnp.maximum(jnp.sum(p, axis=-1, keepdims=True), 1e-30)


def attend_shared_kv(q, k, v, mask, scale):
    s = jnp.einsum('...qhd,...kd->...hqk', q, k) * scale
    p = masked_softmax(s, mask)
    return jnp.einsum('...hqk,...kd->...qhd', p.astype(v.dtype), v), p


def query_block(n):
    return QBLOCK if n % QBLOCK == 0 else n


def split_query_blocks(t, qb):
    b, n = t.shape[:2]
    return jnp.moveaxis(t.reshape((b, n // qb, qb) + t.shape[2:]), 1, 0)


def merge_query_blocks(t):
    t = jnp.moveaxis(t, 0, 1)
    return t.reshape((t.shape[0], t.shape[1] * t.shape[2]) + t.shape[3:])


def gather_pages(cache, layer, page_table):
    pool = cache.reshape((cache.shape[0] * cache.shape[1],) + cache.shape[2:])
    g = pool[page_table + layer * cache.shape[1]]
    return g.reshape((g.shape[0], g.shape[1] * g.shape[2]) + g.shape[3:])


def causal_conv(u, buf, w, b):
    K = w.shape[0]
    L = u.shape[1]
    up = jnp.concatenate([buf.astype(u.dtype), u], axis=1)
    out = b + sum(w[k] * up[:, k:k + L] for k in range(K))
    return out, up[:, up.shape[1] - (K - 1):]


def ssd_scan(x, dt, A, Bm, Cm, h0):
    b, l = x.shape[:2]
    T = SSM_CHUNK if l % SSM_CHUNK == 0 else l
    nc = l // T
    chunk = lambda t: t.reshape((b, nc, T) + t.shape[2:])
    xdt = chunk(x * dt[..., None])
    Bc, Cc = chunk(Bm), chunk(Cm)
    a_cum = jnp.cumsum(chunk(dt * A), axis=2)
    ac = jnp.moveaxis(a_cum, 2, -1)
    causal = jnp.tril(jnp.ones((T, T), dtype=bool))
    decay = jnp.exp(jnp.where(causal, ac[..., :, None] - ac[..., None, :], -jnp.inf))
    y_diag = jnp.einsum('bcign,bcjgn,bcgrij,bcjgrp->bcigrp', Cc, Bc, decay, xdt)
    decay_end = jnp.exp(a_cum[:, :, -1:] - a_cum)
    chunk_states = jnp.einsum('bcjgn,bcjgr,bcjgrp->bcgrpn', Bc, decay_end, xdt)
    chunk_decay = jnp.exp(a_cum[:, :, -1])

    def step(h, inp):
        st, dec = inp
        return dec[..., None, None] * h + st, h

    h_final, h_prev = lax.scan(step, h0.astype(chunk_states.dtype),
                               (jnp.moveaxis(chunk_states, 1, 0), jnp.moveaxis(chunk_decay, 1, 0)))
    h_prev = jnp.moveaxis(h_prev, 0, 1)
    y_off = jnp.einsum('bcign,bcgrpn,bcigr->bcigrp', Cc, h_prev, jnp.exp(a_cum))
    return (y_diag + y_off).reshape(x.shape), h_final


def ssm_mixer(z, xbc, dt_raw, conv_buf, h0, conv_w, conv_b, dt_bias, a_log, d_skip, g_norm):
    b, l = z.shape[:2]
    R = SSM_HEADS // SSM_GROUPS
    xbc, new_buf = causal_conv(xbc, conv_buf, conv_w, conv_b)
    xbc = jax.nn.silu(xbc)
    xs, Bm, Cm = jnp.split(xbc, [SSM_INNER, SSM_INNER + SSM_GROUPS * SSM_DSTATE], axis=-1)
    xs = xs.reshape(b, l, SSM_GROUPS, R, SSM_HEADDIM)
    Bm = Bm.reshape(b, l, SSM_GROUPS, SSM_DSTATE)
    Cm = Cm.reshape(b, l, SSM_GROUPS, SSM_DSTATE)
    dt = jax.nn.softplus((dt_raw + dt_bias).astype(F32)).reshape(b, l, SSM_GROUPS, R)
    A = -jnp.exp(a_log.astype(F32)).reshape(SSM_GROUPS, R)
    y, h_final = ssd_scan(xs, dt, A, Bm, Cm, h0.reshape(b, SSM_GROUPS, R, SSM_HEADDIM, SSM_DSTATE))
    y = y + d_skip.reshape(SSM_GROUPS, R)[:, :, None] * xs
    y = rmsnorm(y.reshape(b, l, SSM_INNER).astype(z.dtype) * jax.nn.silu(z), g_norm)
    return y, h_final.reshape(b, SSM_HEADS, SSM_HEADDIM, SSM_DSTATE), new_buf


def mla_attend(q_lat, q_pe, ckv, kpe, q_pos, k_pos):
    qb = query_block(q_lat.shape[1])
    scale = MLA_QK ** -0.5

    def one(args):
        ql, qp, qpos = args
        s = jnp.einsum('bqhk,bsk->bhqs', ql, ckv) + jnp.einsum('bqhr,bsr->bhqs', qp, kpe)
        p = masked_softmax(s * scale, k_pos[None, :] <= qpos[:, None])
        return jnp.einsum('bhqs,bsk->bqhk', p.astype(ckv.dtype), ckv)

    out = lax.map(one, (split_query_blocks(q_lat, qb), split_query_blocks(q_pe, qb), q_pos.reshape(-1, qb)))
    return merge_query_blocks(out)


def mla_mixer(cq, ckv, kpe, pos, past_ckv, past_kpe, g_q_a, w_uq, g_q, g_kv_a, g_kpe, w_uk, w_uv):
    b, l = cq.shape[:2]
    q = (rmsnorm(cq, g_q_a) @ w_uq).reshape(b, l, MLA_HEADS, MLA_QK)
    q = rmsnorm(q, g_q)
    q_nope, q_pe = q[..., :MLA_NOPE], rope(q[..., MLA_NOPE:], pos)
    ckv = rmsnorm(ckv, g_kv_a)
    kpe = rope(rmsnorm(kpe, g_kpe)[:, :, None, :], pos)[:, :, 0, :]
    if past_ckv is None:
        keys_ckv, keys_kpe, k_pos = ckv, kpe, pos
    else:
        keys_ckv = jnp.concatenate([past_ckv.astype(ckv.dtype), ckv], axis=1)
        keys_kpe = jnp.concatenate([past_kpe.astype(kpe.dtype), kpe], axis=1)
        k_pos = jnp.arange(keys_ckv.shape[1], dtype=jnp.int32)
    q_lat = jnp.einsum('blhd,khd->blhk', q_nope, w_uk)
    o_lat = mla_attend(q_lat, q_pe, keys_ckv, keys_kpe, pos, k_pos)
    o = jnp.einsum('blhk,khv->blhv', o_lat, w_uv).reshape(b, l, MLA_HEADS * MLA_V)
    return o, ckv, kpe


def compress(rows, pos_emb, w1, w2):
    b, T = rows.shape[:2]
    nc = T // CMP_STRIDE
    chunks = rows[:, :nc * CMP_STRIDE].reshape(b, nc, CMP_STRIDE, 2, NSA_D)
    w1a, w1b = w1[:, :CMP_STRIDE], w1[:, CMP_STRIDE:]
    first = jnp.einsum('bnskd,ksde->bnke', chunks, w1a)[:, :-1]
    second = jnp.einsum('bnskd,ksde->bnke', chunks, w1b)[:, 1:]
    pos_term = jnp.einsum('skd,ksde->ke', pos_emb, w1)
    h = jax.nn.gelu(first + second + pos_term)
    out = jnp.einsum('bnke,ked->bnkd', h, w2)
    return out[:, :, 0], out[:, :, 1]


def slc_importance(p_cmp, n_slc):
    b, lq, n_cmp = p_cmp.shape
    R = SLC_BLOCK // CMP_STRIDE
    pp = jnp.pad(p_cmp, ((0, 0), (0, 0), (1, R * (n_slc + 1) - 1 - n_cmp))).reshape(b, lq, n_slc + 1, R)
    return pp[..., :-1, 0] + 2.0 * jnp.sum(pp[..., :-1, 1:], axis=-1) + pp[..., 1:, 0]


def select_blocks(imp, pos, n_slc):
    j = jnp.arange(n_slc, dtype=jnp.int32)[None, :]
    cur = (pos // SLC_BLOCK)[:, None]
    valid = j <= cur
    forced = (j == 0) | (j >= cur - (N_LOCAL - 1))
    score = jnp.where(valid, imp + jnp.where(forced, FORCE_BONUS, 0.0), -jnp.inf)
    vals, idx = lax.top_k(score, min(SLC_TOPN, n_slc))
    return idx, jnp.isfinite(vals)


def slc_attend(q, slc_blocks, pos, idx, sel_valid, scale):
    b, lq = q.shape[:2]
    qb = query_block(lq)
    n_sel = idx.shape[-1]
    offs = jnp.arange(SLC_BLOCK, dtype=jnp.int32)

    def one(args):
        qq, pp, ii, vv = args
        g = jax.vmap(lambda blk, ix: blk[ix])(slc_blocks, ii)
        g = g.reshape(b, qb, n_sel * SLC_BLOCK, 2, NSA_D)
        kpos = (ii[..., None] * SLC_BLOCK + offs).reshape(b, qb, n_sel * SLC_BLOCK)
        mask = jnp.repeat(vv, SLC_BLOCK, axis=-1) & (kpos <= pp[None, :, None])
        o, _ = attend_shared_kv(qq[:, :, None], g[:, :, :, 0], g[:, :, :, 1], mask[:, :, None, None, :], scale)
        return o[:, :, 0]

    out = lax.map(one, (split_query_blocks(q, qb), pos.reshape(-1, qb),
                        split_query_blocks(idx, qb), split_query_blocks(sel_valid, qb)))
    return merge_query_blocks(out)


def window_attend_banded(q, k, v, pos, scale):
    b, l = q.shape[:2]
    qb = query_block(l)
    nb = l // qb
    nw = -(-WINDOW // qb)
    pad = lambda t: jnp.pad(t, ((0, 0), (nw * qb, 0), (0, 0))).reshape(b, nw + nb, qb, NSA_D)
    band = jnp.arange(nb)[:, None] + jnp.arange(nw + 1)[None, :]
    kb = pad(k)[:, band].reshape(b, nb, (nw + 1) * qb, NSA_D)
    vb = pad(v)[:, band].reshape(b, nb, (nw + 1) * qb, NSA_D)
    kpos = (band[..., None] * qb + jnp.arange(qb)).reshape(nb, -1) - nw * qb
    dist = pos.reshape(nb, qb)[:, :, None] - kpos[:, None, :]
    mask = (dist >= 0) & (dist < WINDOW) & (kpos[:, None, :] >= 0)
    o, _ = attend_shared_kv(q.reshape(b, nb, qb, NSA_HEADS, NSA_D), kb, vb, mask[None, :, None], scale)
    return o.reshape(b, l, NSA_HEADS, NSA_D)


def nsa_mixer(q, kv, gates, pos, past_cmp, past_slc, win_buf, g_q, g_k, cmp_pos, cmp_w1, cmp_w2):
    b, l = q.shape[:2]
    scale = NSA_D ** -0.5
    q = rmsnorm(q.reshape(b, l, NSA_HEADS, NSA_D), g_q)
    kv = kv.reshape(b, l, 3, 2, NSA_D)
    cmp_rows = kv[:, :, 0]
    slc_rows = jnp.stack([rmsnorm(kv[:, :, 1, 0], g_k[1]), kv[:, :, 1, 1]], axis=2)
    win_rows = jnp.stack([rmsnorm(kv[:, :, 2, 0], g_k[2]), kv[:, :, 2, 1]], axis=2)
    if past_cmp is None:
        cmp_all, slc_all = cmp_rows, slc_rows
    else:
        cmp_all = jnp.concatenate([past_cmp.astype(cmp_rows.dtype), cmp_rows], axis=1)
        slc_all = jnp.concatenate([past_slc.astype(slc_rows.dtype), slc_rows], axis=1)
    kc, vc = compress(cmp_all, cmp_pos, cmp_w1, cmp_w2)
    kc = rmsnorm(kc, g_k[0])
    cmp_end = jnp.arange(kc.shape[1], dtype=jnp.int32) * CMP_STRIDE + CMP_BLOCK - 1
    o_cmp, p_cmp = attend_shared_kv(q, kc, vc, (cmp_end[None, :] <= pos[:, None])[None, None], scale)
    T = slc_all.shape[1]
    n_slc = -(-T // SLC_BLOCK)
    imp = slc_importance(jnp.sum(p_cmp, axis=1), n_slc)
    idx, sel_valid = select_blocks(imp, pos, n_slc)
    if past_cmp is None:
        sel = jnp.any((idx[..., None] == jnp.arange(n_slc, dtype=jnp.int32)) & sel_valid[..., None], axis=-2)
        o_slc = slc_attend_dense(q, slc_all[:, :, 0], slc_all[:, :, 1], sel, scale)
    else:
        slc_pad = jnp.pad(slc_all, ((0, 0), (0, n_slc * SLC_BLOCK - T), (0, 0), (0, 0)))
        o_slc = slc_attend(q, slc_pad.reshape(b, n_slc, SLC_BLOCK, 2, NSA_D), pos, idx, sel_valid, scale)
    if win_buf is None:
        o_win = window_attend_banded(q, win_rows[:, :, 0], win_rows[:, :, 1], pos, scale)
        new_win = win_rows[:, max(l - WINDOW, 0):]
    else:
        wb = win_buf.shape[1]
        rows = jnp.concatenate([win_buf.astype(win_rows.dtype), win_rows], axis=1)
        kpos = jnp.arange(rows.shape[1], dtype=jnp.int32) + (pos[0] - wb)
        dist = pos[:, None] - kpos[None, :]
        mask = (dist >= 0) & (dist < WINDOW)
        o_win, _ = attend_shared_kv(q, rows[:, :, 0], rows[:, :, 1], mask[None, None], scale)
        new_win = rows[:, rows.shape[1] - wb:]
    g = jax.nn.sigmoid(gates.reshape(b, l, 3, NSA_HEADS).astype(F32))[..., None]
    o = g[:, :, 0] * o_cmp + g[:, :, 1] * o_slc + g[:, :, 2] * o_win
    return o.reshape(b, l, NSA_HEADS * NSA_D).astype(q.dtype), cmp_rows, slc_rows, new_win


def mixers(proj, b, l, pos, past, p):
    def sec(name):
        o = _DST_OFF[name]
        return proj[:, o:o + _SEC[name]].reshape(b, l, _SEC[name])

    if past is None:
        past_ckv = past_kpe = past_cmp = past_slc = win_buf = None
        h0 = jnp.zeros((b, SSM_HEADS, SSM_HEADDIM, SSM_DSTATE), F32)
        conv_buf = jnp.zeros((b, SSM_CONV - 1, SSM_CONV_DIM), F32)
    else:
        past_ckv, past_kpe, past_cmp, past_slc, win_buf, h0, conv_buf = past
    o_ssm, h_new, conv_new = ssm_mixer(sec('z'), sec('xbc'), sec('dt'), conv_buf, h0, p['ssm_conv_w'],
                                       p['ssm_conv_b'], p['ssm_dt_bias'], p['ssm_a_log'], p['ssm_d'],
                                       p['ssm_g_norm'])
    o_mla, ckv_new, kpe_new = mla_mixer(sec('cq'), sec('ckv'), sec('kpe'), pos, past_ckv, past_kpe,
                                        p['mla_g_q_a'], p['mla_w_uq'], p['mla_g_q'], p['mla_g_kv_a'],
                                        p['mla_g_kpe'], p['mla_w_uk'], p['mla_w_uv'])
    o_nsa, cmp_new, slc_new, win_new = nsa_mixer(sec('nq'), sec('nkv'), sec('ngate'), pos, past_cmp, past_slc,
                                                 win_buf, p['nsa_g_q'], p['nsa_g_k'], p['nsa_cmp_pos'],
                                                 p['nsa_cmp_w1'], p['nsa_cmp_w2'])
    br = jnp.stack([o_ssm.reshape(b * l, MIX_W), o_mla.reshape(b * l, MIX_W), o_nsa.reshape(b * l, MIX_W)], axis=0)
    return br, (ckv_new, kpe_new, cmp_new, slc_new, win_new, h_new, conv_new)


def _regroup_w_in(w):
    parts = [w[:, _SRC_OFF[n]:_SRC_OFF[n] + _SEC[n]] for n in _DST_ORDER]
    wr = jnp.concatenate(parts, axis=1).astype(BF16)
    return jnp.pad(wr, ((0, 0), (0, IN_COLS_PAD - wr.shape[1])))


def kernel(x_prompt, x_sample, cache_mla_ckv, cache_mla_kpe, cache_nsa_cmp, cache_nsa_slc, state_nsa_win, state_ssm, state_conv, page_table, c_prompt, c_sample, w_ada, b_ada, g_norm_mix, g_norm_ffn, w_in, ssm_conv_w, ssm_conv_b, ssm_dt_bias, ssm_a_log, ssm_d, ssm_g_norm, mla_g_q_a, mla_w_uq, mla_g_q, mla_g_kv_a, mla_g_kpe, mla_w_uk, mla_w_uv, nsa_g_q, nsa_g_k, nsa_cmp_pos, nsa_cmp_w1, nsa_cmp_w2, w_branch, w_out, moe_w_gu, moe_w_down, w_router, router_bias):
    B, L, D = x_prompt.shape
    S = x_sample.shape[0]
    TP = B * L
    T = TP + S
    past_len = page_table.shape[1] * cache_mla_ckv.shape[2]
    pos_p = jnp.arange(L, dtype=jnp.int32)
    pos_s = past_len + jnp.arange(x_sample.shape[1], dtype=jnp.int32)
    TR = 256
    TM = 640
    assert T % TM == 0 and L % TR == 0 and S % 8 == 0

    nb = B + S
    nb_pad = -(-nb // 8) * 8
    c_all = jnp.pad(jax.nn.silu(jnp.concatenate([c_prompt, c_sample], axis=0)), ((0, nb_pad - nb), (0, 0)))
    mods = [matmul(c_all, w_ada[i], nb_pad, 512, bias=b_ada[i], name='ada') for i in range(DEPTH)]

    def mod_parts(i):
        m = mods[i].reshape(nb_pad, 6, D)
        return [(m[:B, k], m[B:nb, k]) for k in range(6)]

    def rows2(xp, xs, dp, ds, ga, g, sc, sh, wr, name):
        op = rowwise(xp, dp, None if ga is None else ga[0], g, None if sc is None else sc[0],
                     None if sh is None else sh[0], wr, L, TR, name + '_p')
        os_ = rowwise(xs, ds, None if ga is None else ga[1], g, None if sc is None else sc[1],
                      None if sh is None else sh[1], wr, 1, S, name + '_s')
        return op, os_

    xp = x_prompt.reshape(TP, D)
    xs = x_sample.reshape(S, D)
    sh1, sc1, ga1, sh2, sc2, ga2 = mod_parts(0)
    (hn_p,), (hn_s,) = rows2(xp, xs, None, None, None, g_norm_mix[0], sc1, sh1, None, 'norm0')
    sp, ss = [], []
    for i in range(DEPTH):
        p = {
            'ssm_conv_w': ssm_conv_w[i], 'ssm_conv_b': ssm_conv_b[i],
            'ssm_dt_bias': ssm_dt_bias[i], 'ssm_a_log': ssm_a_log[i], 'ssm_d': ssm_d[i],
            'ssm_g_norm': ssm_g_norm[i], 'mla_g_q_a': mla_g_q_a[i], 'mla_w_uq': mla_w_uq[i],
            'mla_g_q': mla_g_q[i], 'mla_g_kv_a': mla_g_kv_a[i], 'mla_g_kpe': mla_g_kpe[i],
            'mla_w_uk': mla_w_uk[i], 'mla_w_uv': mla_w_uv[i], 'nsa_g_q': nsa_g_q[i], 'nsa_g_k': nsa_g_k[i],
            'nsa_cmp_pos': nsa_cmp_pos[i], 'nsa_cmp_w1': nsa_cmp_w1[i], 'nsa_cmp_w2': nsa_cmp_w2[i],
        }
        hn = jnp.concatenate([hn_p, hn_s], axis=0)
        proj = matmul(hn, _regroup_w_in(w_in[i]), TM, IN_TN, name='in_proj')
        br_p, st_p = mixers(proj[:TP], B, L, pos_p, None, p)
        past = (gather_pages(cache_mla_ckv, i, page_table), gather_pages(cache_mla_kpe, i, page_table),
                gather_pages(cache_nsa_cmp, i, page_table), gather_pages(cache_nsa_slc, i, page_table),
                state_nsa_win[i], state_ssm[i], state_conv[i])
        br_s, st_s = mixers(proj[TP:], S, 1, pos_s, past, p)
        sp.append(st_p)
        ss.append(st_s)
        branches = jnp.concatenate([br_p, br_s], axis=1).astype(BF16)
        mixin = branch_mix(branches, w_branch[i].astype(BF16), proj, TM, 512)
        mix = matmul(mixin, w_out[i].astype(BF16), TM, 512, name='out_proj')
        (x1_p, h2_p, lg_p), (x1_s, h2_s, lg_s) = rows2(xp, xs, mix[:TP], mix[TP:], ga1, g_norm_ffn[i], sc2, sh2,
                                                       w_router, 'ffn_norm')
        moe = moe_ffn(jnp.concatenate([h2_p, h2_s], axis=0), jnp.concatenate([lg_p, lg_s], axis=0),
                      router_bias, moe_w_gu[i].astype(BF16), moe_w_down[i].astype(BF16))
        if i + 1 < DEPTH:
            ga2_cur = ga2
            sh1, sc1, ga1, sh2, sc2, ga2 = mod_parts(i + 1)
            (xp, hn_p), (xs, hn_s) = rows2(x1_p, x1_s, moe[:TP], moe[TP:], ga2_cur, g_norm_mix[i + 1], sc1, sh1,
                                           None, 'mix_norm')
        else:
            (xp,), (xs,) = rows2(x1_p, x1_s, moe[:TP], moe[TP:], ga2, None, None, None, None, 'final_add')

    def st(outs, j):
        return jnp.stack([o[j] for o in outs], axis=0)

    return (xp.reshape(B, L, D), xs.reshape(S, 1, D), st(sp, 0), st(ss, 0), st(sp, 1), st(ss, 1), st(sp, 2),
            st(ss, 2), st(sp, 3), st(ss, 3), st(sp, 4), st(ss, 4), st(sp, 5), st(ss, 5), st(sp, 6), st(ss, 6))
```

```python
import functools
import math

import jax
import jax.numpy as jnp
from jax import lax
from jax.experimental import pallas as pl
from jax.experimental.pallas import tpu as pltpu
import numpy as np

D_MODEL = 4096
DEPTH = 2
EPS = 1e-6
QBLOCK = 128
MIX_W = D_MODEL // 2
N_BRANCH = 3
SSM_HEADDIM = 64
SSM_HEADS = MIX_W // SSM_HEADDIM
SSM_INNER = SSM_HEADS * SSM_HEADDIM
SSM_GROUPS = 4
SSM_DSTATE = 128
SSM_CONV = 4
SSM_CHUNK = 128
SSM_CONV_DIM = SSM_INNER + 2 * SSM_GROUPS * SSM_DSTATE
MLA_HEADS = 16
MLA_NOPE = 128
MLA_ROPE = 64
MLA_QK = MLA_NOPE + MLA_ROPE
MLA_V = MIX_W // MLA_HEADS
MLA_Q_LORA = 768
MLA_KV_LORA = 256
ROPE_THETA = 10000.0
NSA_D = 64
NSA_HEADS = MIX_W // NSA_D
CMP_STRIDE = 16
CMP_BLOCK = 2 * CMP_STRIDE
CMP_HIDDEN = 256
SLC_BLOCK = 64
SLC_TOPN = 16
N_LOCAL = 2
FORCE_BONUS = 1e6
WINDOW = 512
N_EXPERTS = 16
N_GROUPS = 4
EXPERTS_PER_GROUP = N_EXPERTS // N_GROUPS
GROUP_SCORE_TOPK = 2
TOP_K = 2
D_FF = 768

LANE = 128
VMEM_LIMIT = 56 * 1024 * 1024

BF16 = jnp.bfloat16
F32 = jnp.float32

_SEC = dict(z=SSM_INNER, xbc=SSM_CONV_DIM, dt=SSM_HEADS, cq=MLA_Q_LORA, ckv=MLA_KV_LORA, kpe=MLA_ROPE,
            nq=NSA_HEADS * NSA_D, nkv=6 * NSA_D, ngate=3 * NSA_HEADS, bgate=N_BRANCH * D_MODEL)
_SRC_ORDER = ('z', 'xbc', 'dt', 'cq', 'ckv', 'kpe', 'nq', 'nkv', 'ngate', 'bgate')
_DST_ORDER = ('bgate', 'z', 'xbc', 'cq', 'ckv', 'nq', 'nkv', 'dt', 'kpe', 'ngate')
_SRC_OFF = {}
_o = 0
for _n in _SRC_ORDER:
    _SRC_OFF[_n] = _o
    _o += _SEC[_n]
IN_COLS = _o
_DST_OFF = {}
_o = 0
for _n in _DST_ORDER:
    _DST_OFF[_n] = _o
    _o += _SEC[_n]
IN_TN = 512
IN_COLS_PAD = -(-_o // IN_TN) * IN_TN


def _cparams(sem):
    return pltpu.CompilerParams(dimension_semantics=sem, vmem_limit_bytes=VMEM_LIMIT)


def _mm_body(x_ref, w_ref, o_ref):
    o_ref[...] = jnp.dot(x_ref[...].astype(BF16), w_ref[...].astype(BF16), preferred_element_type=F32)


def _mm_bias_body(x_ref, w_ref, b_ref, o_ref):
    o_ref[...] = jnp.dot(x_ref[...].astype(BF16), w_ref[...].astype(BF16),
                         preferred_element_type=F32) + b_ref[...]


def matmul(x, w, tm, tn, bias=None, name='mm', rows=None, layer=None):
    _, K = x.shape
    N = w.shape[-1]
    r0, M = (0, x.shape[0]) if rows is None else rows
    assert M % tm == 0 and r0 % tm == 0 and N % tn == 0, (M, N, tm, tn)
    i0 = r0 // tm
    if layer is None:
        w_spec = pl.BlockSpec((K, tn), lambda i, j: (0, j))
    else:
        w_spec = pl.BlockSpec((None, K, tn), lambda i, j: (layer, 0, j))
    in_specs = [pl.BlockSpec((tm, K), lambda i, j: (i + i0, 0)), w_spec]
    args = [x, w]
    body = _mm_body
    if bias is not None:
        in_specs.append(pl.BlockSpec((1, tn), lambda i, j: (0, j)))
        args.append(bias.reshape(1, N))
        body = _mm_bias_body
    return pl.pallas_call(
        body, grid=(M // tm, N // tn), in_specs=in_specs,
        out_specs=pl.BlockSpec((tm, tn), lambda i, j: (i, j)),
        out_shape=jax.ShapeDtypeStruct((M, N), F32),
        compiler_params=_cparams(('parallel', 'parallel')), name=name)(*args)


def _row_body(*refs, has_delta, has_norm, has_router):
    it = iter(refs)
    x_ref = next(it)
    if has_delta:
        d_ref, ga_ref = next(it), next(it)
    if has_norm:
        g_ref, sc_ref, sh_ref = next(it), next(it), next(it)
    if has_router:
        wr_ref = next(it)
    x = x_ref[...]
    if has_delta:
        x = x + ga_ref[...] * d_ref[...]
        xo_ref = next(it)
        xo_ref[...] = x
    if has_norm:
        hn_ref = next(it)
        y = x * lax.rsqrt(jnp.mean(x * x, axis=-1, keepdims=True) + EPS)
        hn = y * g_ref[...] * (1.0 + sc_ref[...]) + sh_ref[...]
        hn_ref[...] = hn.astype(BF16)
        if has_router:
            lg_ref = next(it)
            lg_ref[...] = jnp.dot(hn, wr_ref[...], preferred_element_type=F32,
                                  precision=lax.Precision.HIGHEST)


def rowwise(x, delta, ga, g, sc, sh, w_router, rows_per_mod, tr, name):
    R, D = x.shape
    has_delta = delta is not None
    has_norm = g is not None
    has_router = w_router is not None
    assert R % tr == 0
    row_spec = pl.BlockSpec((tr, D), lambda i: (i, 0))
    if rows_per_mod == 1:
        mod_spec = row_spec
        mod_arr = lambda a: a
    else:
        assert rows_per_mod % tr == 0
        k = rows_per_mod // tr
        mod_spec = pl.BlockSpec((None, 1, D), lambda i: (i // k, 0, 0))
        mod_arr = lambda a: a.reshape(a.shape[0], 1, D)
    args, in_specs = [x], [row_spec]
    if has_delta:
        args += [delta, mod_arr(ga)]
        in_specs += [row_spec, mod_spec]
    if has_norm:
        args += [g.reshape(1, D), mod_arr(sc), mod_arr(sh)]
        in_specs += [pl.BlockSpec((1, D), lambda i: (0, 0)), mod_spec, mod_spec]
    if has_router:
        args.append(w_router)
        in_specs.append(pl.BlockSpec((D, N_EXPERTS), lambda i: (0, 0)))
    out_shape, out_specs = [], []
    if has_delta:
        out_shape.append(jax.ShapeDtypeStruct((R, D), F32))
        out_specs.append(row_spec)
    if has_norm:
        out_shape.append(jax.ShapeDtypeStruct((R, D), BF16))
        out_specs.append(row_spec)
        if has_router:
            out_shape.append(jax.ShapeDtypeStruct((R, N_EXPERTS), F32))
            out_specs.append(pl.BlockSpec((tr, N_EXPERTS), lambda i: (i, 0)))
    body = functools.partial(_row_body, has_delta=has_delta, has_norm=has_norm, has_router=has_router)
    return pl.pallas_call(body, grid=(R // tr,), in_specs=in_specs, out_specs=out_specs, out_shape=out_shape,
                          compiler_params=_cparams(('parallel',)), name=name)(*args)


def _bmix_body(br_ref, w_ref, g_ref, o_ref):
    m = pl.program_id(2)
    p = jnp.dot(br_ref[...], w_ref[...], preferred_element_type=F32)
    t = jax.nn.sigmoid(g_ref[...]) * p

    @pl.when(m == 0)
    def _():
        o_ref[...] = t

    @pl.when(m > 0)
    def _():
        o_ref[...] += t


def branch_mix(branches, w_branch, proj, tm, tn):
    _, T, K = branches.shape
    g0 = _DST_OFF['bgate'] // tn
    assert _DST_OFF['bgate'] % tn == 0 and T % tm == 0
    nj = D_MODEL // tn
    return pl.pallas_call(
        _bmix_body, grid=(T // tm, nj, N_BRANCH),
        in_specs=[pl.BlockSpec((None, tm, K), lambda i, j, m: (m, i, 0)),
                  pl.BlockSpec((None, K, tn), lambda i, j, m: (m, 0, j)),
                  pl.BlockSpec((tm, tn), lambda i, j, m: (i, g0 + m * nj + j))],
        out_specs=pl.BlockSpec((tm, tn), lambda i, j, m: (i, j)),
        out_shape=jax.ShapeDtypeStruct((T, D_MODEL), F32),
        compiler_params=_cparams(('parallel', 'parallel', 'arbitrary')), name='branch_mix')(branches, w_branch, proj)


MOE_TM = 512
MOE_FC = 384
MOE_NF = D_FF // MOE_FC


def _moe_body(te_ref, tv_ref, x_ref, wg_ref, wu_ref, wd_ref, rw_ref, o_ref):
    i = pl.program_id(0)
    f = pl.program_id(1)

    @pl.when(tv_ref[i] == 0)
    def _():
        o_ref[...] = jnp.zeros_like(o_ref)

    @pl.when(tv_ref[i] != 0)
    def _():
        x = x_ref[...]
        g = jnp.dot(x, wg_ref[...], preferred_element_type=F32)
        u = jnp.dot(x, wu_ref[...], preferred_element_type=F32)
        a = (jax.nn.silu(g) * u * rw_ref[...]).astype(BF16)
        t = jnp.dot(a, wd_ref[...], preferred_element_type=F32)

        @pl.when(f == 0)
        def _():
            o_ref[...] = t

        @pl.when(f > 0)
        def _():
            o_ref[...] += t


def moe_grouped(xs, row_w, tile_e, tile_v, w_gu, w_down):
    P, D = xs.shape
    nt = P // MOE_TM
    grid_spec = pltpu.PrefetchScalarGridSpec(
        num_scalar_prefetch=2, grid=(nt, MOE_NF),
        in_specs=[pl.BlockSpec((MOE_TM, D), lambda i, f, te, tv: (i, 0)),
                  pl.BlockSpec((None, D, MOE_FC), lambda i, f, te, tv: (te[i], 0, f)),
                  pl.BlockSpec((None, D, MOE_FC), lambda i, f, te, tv: (te[i], 0, f + MOE_NF)),
                  pl.BlockSpec((None, MOE_FC, D), lambda i, f, te, tv: (te[i], f, 0)),
                  pl.BlockSpec((MOE_TM, 1), lambda i, f, te, tv: (i, 0))],
        out_specs=pl.BlockSpec((MOE_TM, D), lambda i, f, te, tv: (i, 0)))
    return pl.pallas_call(
        _moe_body, grid_spec=grid_spec, out_shape=jax.ShapeDtypeStruct((P, D), F32),
        compiler_params=_cparams(('parallel', 'arbitrary')), name='moe_grouped')(
            tile_e, tile_v, xs, w_gu, w_gu, w_down, row_w)


def moe_route(logits, router_bias):
    scores = jax.nn.sigmoid(logits)
    biased = scores + router_bias
    assert GROUP_SCORE_TOPK == 2 and TOP_K == 2
    g4 = biased.reshape(-1, N_GROUPS, EXPERTS_PER_GROUP)
    pair_sums = [g4[..., i] + g4[..., j] for i in range(EXPERTS_PER_GROUP) for j in range(i + 1, EXPERTS_PER_GROUP)]
    grp = functools.reduce(jnp.maximum, pair_sums)
    best = jnp.argmax(grp, axis=-1)
    lane = jnp.arange(N_EXPERTS, dtype=jnp.int32)[None, :]
    masked = jnp.where(lane // EXPERTS_PER_GROUP == best[:, None], biased, -jnp.inf)
    i1 = jnp.argmax(masked, axis=-1).astype(jnp.int32)
    i2 = jnp.argmax(jnp.where(lane == i1[:, None], -jnp.inf, masked), axis=-1).astype(jnp.int32)
    idx = jnp.stack([i1, i2], axis=-1)
    w = jnp.take_along_axis(scores, idx, axis=-1)
    w = w / jnp.sum(w, axis=-1, keepdims=True)
    return idx.astype(jnp.int32), w


def moe_ffn(h2, logits, router_bias, w_gu, w_down):
    T, D = h2.shape
    idx, w = moe_route(logits, router_bias)
    A = T * TOP_K
    flat_e = idx.reshape(A)
    onehot = (flat_e[:, None] == jnp.arange(N_EXPERTS, dtype=jnp.int32)[None, :]).astype(jnp.int32)
    csum = jnp.cumsum(onehot, axis=0)
    rank = jnp.sum(csum * onehot, axis=1) - 1
    counts = csum[-1]
    padded = (counts + MOE_TM - 1) // MOE_TM * MOE_TM
    end_p = jnp.cumsum(padded)
    start_p = end_p - padded
    dest_flat = jnp.sum(start_p[None, :] * onehot, axis=1) + rank
    P = -(-(A + N_EXPERTS * (MOE_TM - 1)) // MOE_TM) * MOE_TM
    row_src = jnp.zeros((P,), jnp.int32).at[dest_flat].set(jnp.arange(A, dtype=jnp.int32) // TOP_K)
    row_w = jnp.zeros((P,), F32).at[dest_flat].set(w.reshape(A))
    dest = dest_flat.reshape(T, TOP_K)
    nt = P // MOE_TM
    tile_start = jnp.arange(nt, dtype=jnp.int32) * MOE_TM
    tile_e = jnp.minimum(jnp.searchsorted(end_p, tile_start, side='right'), N_EXPERTS - 1).astype(jnp.int32)
    tile_v = (tile_start < end_p[-1]).astype(jnp.int32)
    last_e = tile_e[jnp.maximum(jnp.sum(tile_v) - 1, 0)]
    tile_e = jnp.where(tile_v != 0, tile_e, last_e)
    xs = jnp.take(h2, row_src, axis=0)
    out = moe_grouped(xs, row_w.reshape(P, 1), tile_e, tile_v, w_gu, w_down)
    return jnp.take(out, dest[:, 0], axis=0) + jnp.take(out, dest[:, 1], axis=0)


SLC_TQ = 128
CAUSAL_TK = 512


def _causal_extents(n_keys, tq):
    if n_keys % CAUSAL_TK or CAUSAL_TK % tq:
        return [n_keys], lambda qi: 0
    return [CAUSAL_TK * (c + 1) for c in range(n_keys // CAUSAL_TK)], lambda qi: (qi * tq) // CAUSAL_TK


def _slc_dense_body(sel_ref, q_ref, k_ref, v_ref, e_ref, o_ref, *, scale):
    n_heads, tq, _ = q_ref.shape
    qi = pl.program_id(1)
    extents, case_of = _causal_extents(k_ref.shape[0], tq)

    def run(ext):
        k = k_ref[:ext, :].astype(BF16)
        v = v_ref[:ext, :].astype(BF16)
        selx = jnp.dot(sel_ref[...], e_ref[:, :ext], preferred_element_type=F32)
        row = qi * tq + lax.broadcasted_iota(jnp.int32, (tq, ext), 0)
        col = lax.broadcasted_iota(jnp.int32, (tq, ext), 1)
        bias = jnp.where(col <= row, jnp.where(selx > 0.5, 0.0, -jnp.inf), -jnp.inf)

        def head(h, carry):
            q = (q_ref[h] * scale).astype(BF16)
            s = lax.dot_general(q, k, (((1,), (1,)), ((), ())), preferred_element_type=F32) + bias
            m = jnp.max(s, axis=-1, keepdims=True)
            m = jnp.where(m == -jnp.inf, 0.0, m)
            p = jnp.exp(s - m)
            l = jnp.sum(p, axis=-1, keepdims=True)
            o = jnp.dot(p.astype(BF16), v, preferred_element_type=F32)
            o_ref[h] = o / jnp.maximum(l, 1e-30)
            return carry

        lax.fori_loop(0, n_heads, head, 0)

    for c, ext in enumerate(extents):
        pl.when(case_of(qi) == c)(functools.partial(run, ext))


def slc_attend_dense(q, k, v, sel, scale):
    b, L, H, d = q.shape
    n_blk = sel.shape[-1]
    assert L % SLC_TQ == 0 and n_blk * SLC_BLOCK == L
    expand = (jnp.arange(L, dtype=jnp.int32)[None, :] // SLC_BLOCK
              == jnp.arange(n_blk, dtype=jnp.int32)[:, None]).astype(BF16)
    o = pl.pallas_call(
        functools.partial(_slc_dense_body, scale=scale), grid=(b, L // SLC_TQ),
        in_specs=[pl.BlockSpec((None, SLC_TQ, n_blk), lambda bi, qi: (bi, qi, 0)),
                  pl.BlockSpec((None, H, SLC_TQ, d), lambda bi, qi: (bi, 0, qi, 0)),
                  pl.BlockSpec((None, L, d), lambda bi, qi: (bi, 0, 0)),
                  pl.BlockSpec((None, L, d), lambda bi, qi: (bi, 0, 0)),
                  pl.BlockSpec((n_blk, L), lambda bi, qi: (0, 0))],
        out_specs=pl.BlockSpec((None, H, SLC_TQ, d), lambda bi, qi: (bi, 0, qi, 0)),
        out_shape=jax.ShapeDtypeStruct((b, H, L, d), F32),
        compiler_params=_cparams(('parallel', 'parallel')), name='slc_dense')(
            sel.astype(BF16), jnp.transpose(q, (0, 2, 1, 3)), k, v, expand)
    return jnp.transpose(o, (0, 2, 1, 3))


MLA_TQ = 128


def _mla_prompt_body(ql_ref, qp_ref, ckv_ref, kpe_ref, o_ref, *, scale):
    n_heads, tq, _ = ql_ref.shape
    qi = pl.program_id(1)
    extents, case_of = _causal_extents(ckv_ref.shape[0], tq)
    dn = (((1,), (1,)), ((), ()))

    def run(ext):
        ckv = ckv_ref[:ext, :].astype(BF16)
        kpe = kpe_ref[:ext, :].astype(BF16)
        row = qi * tq + lax.broadcasted_iota(jnp.int32, (tq, ext), 0)
        col = lax.broadcasted_iota(jnp.int32, (tq, ext), 1)
        bias = jnp.where(col <= row, 0.0, -jnp.inf)

        def head(h, carry):
            ql = (ql_ref[h] * scale).astype(BF16)
            qp = (qp_ref[h] * scale).astype(BF16)
            s = (lax.dot_general(ql, ckv, dn, preferred_element_type=F32)
                 + lax.dot_general(qp, kpe, dn, preferred_element_type=F32) + bias)
            m = jnp.max(s, axis=-1, keepdims=True)
            m = jnp.where(m == -jnp.inf, 0.0, m)
            p = jnp.exp(s - m)
            l = jnp.sum(p, axis=-1, keepdims=True)
            o_ref[h] = jnp.dot(p.astype(BF16), ckv, preferred_element_type=F32) / jnp.maximum(l, 1e-30)
            return carry

        lax.fori_loop(0, n_heads, head, 0)

    for c, ext in enumerate(extents):
        pl.when(case_of(qi) == c)(functools.partial(run, ext))


def mla_attend_prompt(q_lat, q_pe, ckv, kpe):
    b, L, H, C = q_lat.shape
    R = q_pe.shape[-1]
    assert L % MLA_TQ == 0
    qspec = lambda w: pl.BlockSpec((None, H, MLA_TQ, w), lambda bi, qi: (bi, 0, qi, 0))
    kspec = lambda w: pl.BlockSpec((None, L, w), lambda bi, qi: (bi, 0, 0))
    o = pl.pallas_call(
        functools.partial(_mla_prompt_body, scale=MLA_QK ** -0.5), grid=(b, L // MLA_TQ),
        in_specs=[qspec(C), qspec(R), kspec(C), kspec(R)], out_specs=qspec(C),
        out_shape=jax.ShapeDtypeStruct((b, H, L, C), F32),
        compiler_params=_cparams(('parallel', 'parallel')), name='mla_prompt')(
            jnp.transpose(q_lat, (0, 2, 1, 3)), jnp.transpose(q_pe, (0, 2, 1, 3)), ckv, kpe)
    return jnp.transpose(o, (0, 2, 1, 3))


MLA_PPS = 16


def _mla_decode_body(pt_ref, ql_ref, qp_ref, cn_ref, kn_ref, *rest, scale):
    ckv_refs = rest[:MLA_PPS]
    kpe_refs = rest[MLA_PPS:2 * MLA_PPS]
    o_ref, m_sc, l_sc, acc_sc = rest[2 * MLA_PPS:]
    c = pl.program_id(1)
    ql = ql_ref[...] * scale
    qp = qp_ref[...] * scale

    @pl.when(c == 0)
    def _():
        cn = cn_ref[...]
        s0 = jnp.sum(ql * cn, axis=-1, keepdims=True) + jnp.sum(qp * kn_ref[...], axis=-1, keepdims=True)
        m_sc[...] = s0
        l_sc[...] = jnp.ones_like(l_sc)
        acc_sc[...] = jnp.broadcast_to(cn, acc_sc.shape)

    qlb = ql.astype(BF16)
    qpb = qp.astype(BF16)
    dn = (((1,), (1,)), ((), ()))
    vals = jnp.concatenate([r[...].astype(BF16) for r in ckv_refs], axis=0)
    pes = jnp.concatenate([r[...].astype(BF16) for r in kpe_refs], axis=0)
    s = (lax.dot_general(qlb, vals, dn, preferred_element_type=F32)
         + lax.dot_general(qpb, pes, dn, preferred_element_type=F32))
    m_old = m_sc[...]
    m_new = jnp.maximum(m_old, jnp.max(s, axis=-1, keepdims=True))
    alpha = jnp.exp(m_old - m_new)
    p = jnp.exp(s - m_new)
    l_sc[...] = alpha * l_sc[...] + jnp.sum(p, axis=-1, keepdims=True)
    acc_sc[...] = alpha * acc_sc[...] + jnp.dot(p.astype(BF16), vals, preferred_element_type=F32)
    m_sc[...] = m_new

    @pl.when(c == pl.num_programs(1) - 1)
    def _():
        o_ref[...] = acc_sc[...] / l_sc[...]


def mla_decode_paged(q_lat, q_pe, ckv_new, kpe_new, cache_ckv, cache_kpe, page_table, layer):
    S, H, C = q_lat.shape
    R = q_pe.shape[-1]
    depth, n_pool, page, _ = cache_ckv.shape
    n_pages = page_table.shape[1]
    assert n_pages % MLA_PPS == 0
    base = layer * n_pool
    pool_ckv = cache_ckv.reshape(depth * n_pool, page, C)
    pool_kpe = cache_kpe.reshape(depth * n_pool, page, R)

    def page_spec(width, k):
        return pl.BlockSpec((None, page, width),
                            lambda b, c, pt, k=k: (pt[b * n_pages + c * MLA_PPS + k] + base, 0, 0))

    per_b = lambda shape: pl.BlockSpec((None,) + shape, lambda b, c, pt: (b, 0, 0))
    grid_spec = pltpu.PrefetchScalarGridSpec(
        num_scalar_prefetch=1, grid=(S, n_pages // MLA_PPS),
        in_specs=[per_b((H, C)), per_b((H, R)), per_b((1, C)), per_b((1, R))]
                 + [page_spec(C, k) for k in range(MLA_PPS)] + [page_spec(R, k) for k in range(MLA_PPS)],
        out_specs=per_b((H, C)),
        scratch_shapes=[pltpu.VMEM((H, 1), F32), pltpu.VMEM((H, 1), F32), pltpu.VMEM((H, C), F32)])
    return pl.pallas_call(
        functools.partial(_mla_decode_body, scale=MLA_QK ** -0.5), grid_spec=grid_spec,
        out_shape=jax.ShapeDtypeStruct((S, H, C), F32),
        compiler_params=_cparams(('parallel', 'arbitrary')), name='mla_decode')(
            page_table.reshape(-1), q_lat, q_pe, ckv_new, kpe_new,
            *([pool_ckv] * MLA_PPS), *([pool_kpe] * MLA_PPS))


def _slc_decode_body(hp_ref, q_ref, kn_ref, vn_ref, bias_ref, bn_ref, *rest, scale, n_sel):
    blk_refs = rest[:n_sel]
    o_ref = rest[n_sel]
    d = q_ref.shape[-1]
    q = q_ref[...] * scale
    rows = jnp.concatenate([r[...] for r in blk_refs], axis=0)
    k = rows[:, :d].astype(BF16)
    v = rows[:, d:].astype(BF16)
    s = lax.dot_general(q.astype(BF16), k, (((1,), (1,)), ((), ())), preferred_element_type=F32) + bias_ref[...]
    s_new = jnp.sum(q * kn_ref[...], axis=-1, keepdims=True) + bn_ref[...]
    m = jnp.maximum(jnp.max(s, axis=-1, keepdims=True), s_new)
    m = jnp.where(m == -jnp.inf, 0.0, m)
    p = jnp.exp(s - m)
    p_new = jnp.exp(s_new - m)
    l = jnp.sum(p, axis=-1, keepdims=True) + p_new
    o = jnp.dot(p.astype(BF16), v, preferred_element_type=F32) + p_new * vn_ref[...]
    o_ref[...] = o / jnp.maximum(l, 1e-30)


def slc_decode_paged(q, k_new, v_new, idx, sel_valid, cache_slc, page_table, layer, scale):
    S, H, d = q.shape
    n_sel = idx.shape[-1]
    depth, n_pool, page, _, _ = cache_slc.shape
    per_page = page // SLC_BLOCK
    n_past_blocks = page_table.shape[1] * per_page
    pool = cache_slc.reshape(depth * n_pool * per_page, SLC_BLOCK, 2 * d)
    in_cache = idx < n_past_blocks
    j = jnp.where(in_cache, idx, 0)
    phys = jnp.take_along_axis(page_table, j // per_page, axis=1) + layer * n_pool
    hp = (phys * per_page + j % per_page).astype(jnp.int32)
    bias = jnp.where(jnp.repeat(sel_valid & in_cache, SLC_BLOCK, axis=-1), 0.0, -jnp.inf).astype(F32)
    bias_new = jnp.where(jnp.any(sel_valid & ~in_cache, axis=-1), 0.0, -jnp.inf).astype(F32)
    per_b = lambda shape: pl.BlockSpec((None,) + shape, lambda b, hp_: (b, 0, 0))
    grid_spec = pltpu.PrefetchScalarGridSpec(
        num_scalar_prefetch=1, grid=(S,),
        in_specs=[per_b((H, d)), per_b((1, d)), per_b((1, d)), per_b((1, n_sel * SLC_BLOCK)), per_b((1, 1))]
                 + [pl.BlockSpec((None, SLC_BLOCK, 2 * d), lambda b, hp_, s=s: (hp_[b * n_sel + s], 0, 0))
                    for s in range(n_sel)],
        out_specs=per_b((H, d)))
    return pl.pallas_call(
        functools.partial(_slc_decode_body, scale=scale, n_sel=n_sel), grid_spec=grid_spec,
        out_shape=jax.ShapeDtypeStruct((S, H, d), F32),
        compiler_params=_cparams(('parallel',)), name='slc_decode')(
            hp.reshape(-1), q, k_new, v_new, bias.reshape(S, 1, n_sel * SLC_BLOCK), bias_new.reshape(S, 1, 1),
            *([pool] * n_sel))


def rmsnorm(x, g):
    xf = x.astype(F32)
    y = xf * lax.rsqrt(jnp.mean(xf * xf, axis=-1, keepdims=True) + EPS)
    return y.astype(x.dtype) * g


def rope(x, pos):
    half = x.shape[-1] // 2
    freqs = ROPE_THETA ** (-jnp.arange(half, dtype=F32) / half)
    ang = pos.astype(F32)[:, None] * freqs[None, :]
    cos = jnp.cos(ang)[None, :, None, :].astype(x.dtype)
    sin = jnp.sin(ang)[None, :, None, :].astype(x.dtype)
    x1, x2 = x[..., :half], x[..., half:]
    return jnp.concatenate([x1 * cos - x2 * sin, x1 * sin + x2 * cos], axis=-1)


def masked_softmax(s, mask):
    s = jnp.where(mask, s.astype(F32), -jnp.inf)
    m = jnp.max(s, axis=-1, keepdims=True)
    m = jnp.where(jnp.isfinite(m), m, 0.0)
    p = jnp.exp(s - m)
    return p / jnp.maximum(jnp.sum(p, axis=-1, keepdims=True), 1e-30)


def attend_shared_kv(q, k, v, mask, scale):
    s = jnp.einsum('...qhd,...kd->...hqk', q, k) * scale
    p = masked_softmax(s, mask)
    return jnp.einsum('...hqk,...kd->...qhd', p.astype(v.dtype), v), p


def query_block(n):
    return QBLOCK if n % QBLOCK == 0 else n


def split_query_blocks(t, qb):
    b, n = t.shape[:2]
    return jnp.moveaxis(t.reshape((b, n // qb, qb) + t.shape[2:]), 1, 0)


def merge_query_blocks(t):
    t = jnp.moveaxis(t, 0, 1)
    return t.reshape((t.shape[0], t.shape[1] * t.shape[2]) + t.shape[3:])


def gather_pages(cache, layer, page_table):
    pool = cache.reshape((cache.shape[0] * cache.shape[1],) + cache.shape[2:])
    g = pool[page_table + layer * cache.shape[1]]
    return g.reshape((g.shape[0], g.shape[1] * g.shape[2]) + g.shape[3:])


def causal_conv(u, buf, w, b):
    K = w.shape[0]
    L = u.shape[1]
    up = jnp.concatenate([buf.astype(u.dtype), u], axis=1)
    out = b + sum(w[k] * up[:, k:k + L] for k in range(K))
    return out, up[:, up.shape[1] - (K - 1):]


def ssd_scan(x, dt, A, Bm, Cm, h0):
    b, l = x.shape[:2]
    T = SSM_CHUNK if l % SSM_CHUNK == 0 else l
    nc = l // T
    chunk = lambda t: t.reshape((b, nc, T) + t.shape[2:])
    xdt = chunk(x * dt[..., None])
    Bc, Cc = chunk(Bm), chunk(Cm)
    a_cum = jnp.cumsum(chunk(dt * A), axis=2)
    ac = jnp.moveaxis(a_cum, 2, -1)
    causal = jnp.tril(jnp.ones((T, T), dtype=bool))
    decay = jnp.exp(jnp.where(causal, ac[..., :, None] - ac[..., None, :], -jnp.inf))
    y_diag = jnp.einsum('bcign,bcjgn,bcgrij,bcjgrp->bcigrp', Cc, Bc, decay, xdt)
    decay_end = jnp.exp(a_cum[:, :, -1:] - a_cum)
    chunk_states = jnp.einsum('bcjgn,bcjgr,bcjgrp->bcgrpn', Bc, decay_end, xdt)
    chunk_decay = jnp.exp(a_cum[:, :, -1])

    def step(h, inp):
        st, dec = inp
        return dec[..., None, None] * h + st, h

    h_final, h_prev = lax.scan(step, h0.astype(chunk_states.dtype),
                               (jnp.moveaxis(chunk_states, 1, 0), jnp.moveaxis(chunk_decay, 1, 0)))
    h_prev = jnp.moveaxis(h_prev, 0, 1)
    y_off = jnp.einsum('bcign,bcgrpn,bcigr->bcigrp', Cc, h_prev, jnp.exp(a_cum))
    return (y_diag + y_off).reshape(x.shape), h_final


def ssm_mixer(z, xbc, dt_raw, conv_buf, h0, conv_w, conv_b, dt_bias, a_log, d_skip, g_norm):
    b, l = z.shape[:2]
    R = SSM_HEADS // SSM_GROUPS
    xbc, new_buf = causal_conv(xbc, conv_buf, conv_w, conv_b)
    xbc = jax.nn.silu(xbc)
    xs, Bm, Cm = jnp.split(xbc, [SSM_INNER, SSM_INNER + SSM_GROUPS * SSM_DSTATE], axis=-1)
    xs = xs.reshape(b, l, SSM_GROUPS, R, SSM_HEADDIM)
    Bm = Bm.reshape(b, l, SSM_GROUPS, SSM_DSTATE)
    Cm = Cm.reshape(b, l, SSM_GROUPS, SSM_DSTATE)
    dt = jax.nn.softplus((dt_raw + dt_bias).astype(F32)).reshape(b, l, SSM_GROUPS, R)
    A = -jnp.exp(a_log.astype(F32)).reshape(SSM_GROUPS, R)
    y, h_final = ssd_scan(xs, dt, A, Bm, Cm, h0.reshape(b, SSM_GROUPS, R, SSM_HEADDIM, SSM_DSTATE))
    y = y + d_skip.reshape(SSM_GROUPS, R)[:, :, None] * xs
    y = rmsnorm(y.reshape(b, l, SSM_INNER).astype(z.dtype) * jax.nn.silu(z), g_norm)
    return y, h_final.reshape(b, SSM_HEADS, SSM_HEADDIM, SSM_DSTATE), new_buf


def mla_attend(q_lat, q_pe, ckv, kpe, q_pos, k_pos):
    qb = query_block(q_lat.shape[1])
    scale = MLA_QK ** -0.5

    def one(args):
        ql, qp, qpos = args
        s = jnp.einsum('bqhk,bsk->bhqs', ql, ckv) + jnp.einsum('bqhr,bsr->bhqs', qp, kpe)
        p = masked_softmax(s * scale, k_pos[None, :] <= qpos[:, None])
        return jnp.einsum('bhqs,bsk->bqhk', p.astype(ckv.dtype), ckv)

    out = lax.map(one, (split_query_blocks(q_lat, qb), split_query_blocks(q_pe, qb), q_pos.reshape(-1, qb)))
    return merge_query_blocks(out)


def mla_mixer(cq, ckv, kpe, pos, paged, g_q_a, w_uq, g_q, g_kv_a, g_kpe, w_uk, w_uv):
    b, l = cq.shape[:2]
    q = (rmsnorm(cq, g_q_a) @ w_uq).reshape(b, l, MLA_HEADS, MLA_QK)
    q = rmsnorm(q, g_q)
    q_nope, q_pe = q[..., :MLA_NOPE], rope(q[..., MLA_NOPE:], pos)
    ckv = rmsnorm(ckv, g_kv_a)
    kpe = rope(rmsnorm(kpe, g_kpe)[:, :, None, :], pos)[:, :, 0, :]
    q_lat = jnp.einsum('blhd,khd->blhk', q_nope, w_uk)
    if paged is None:
        o_lat = mla_attend_prompt(q_lat, q_pe, ckv, kpe)
    else:
        assert l == 1
        cache_ckv, cache_kpe, page_table, layer = paged
        o_lat = mla_decode_paged(q_lat[:, 0], q_pe[:, 0], ckv, kpe, cache_ckv, cache_kpe, page_table, layer)[:, None]
    o = jnp.einsum('blhk,khv->blhv', o_lat, w_uv).reshape(b, l, MLA_HEADS * MLA_V)
    return o, ckv, kpe


def compress(rows, pos_emb, w1, w2):
    b, T = rows.shape[:2]
    nc = T // CMP_STRIDE
    chunks = rows[:, :nc * CMP_STRIDE].reshape(b, nc, CMP_STRIDE, 2, NSA_D)
    w1a, w1b = w1[:, :CMP_STRIDE], w1[:, CMP_STRIDE:]
    first = jnp.einsum('bnskd,ksde->bnke', chunks, w1a)[:, :-1]
    second = jnp.einsum('bnskd,ksde->bnke', chunks, w1b)[:, 1:]
    pos_term = jnp.einsum('skd,ksde->ke', pos_emb, w1)
    h = jax.nn.gelu(first + second + pos_term)
    out = jnp.einsum('bnke,ked->bnkd', h, w2)
    return out[:, :, 0], out[:, :, 1]


CMP_POOL_TM = 1024


def compress_paged(cache_cmp, page_table, layer, pos_emb, w1, w2):
    depth, n_pool, page, _, d = cache_cmp.shape
    per_page = page // CMP_STRIDE
    width = CMP_STRIDE * 2 * d
    hid = w1.shape[-1]
    assert page % CMP_STRIDE == 0 and (n_pool * per_page) % CMP_POOL_TM == 0
    eye = jnp.eye(2, dtype=w1.dtype)
    wa = jnp.einsum('ksde,kj->skdje', w1[:, :CMP_STRIDE], eye)
    wb = jnp.einsum('ksde,kj->skdje', w1[:, CMP_STRIDE:], eye)
    wcat = jnp.stack([wa, wb], axis=3).reshape(width, 2 * 2 * hid).astype(BF16)
    pool = cache_cmp.reshape(depth * n_pool * per_page, width)
    prod = matmul(pool, wcat, CMP_POOL_TM, 512, name='cmp_pool', rows=(layer * n_pool * per_page, n_pool * per_page))
    g = prod.reshape(n_pool, per_page * 2 * 2 * hid)[page_table]
    S, n_pages = page_table.shape
    g = g.reshape(S, n_pages * per_page, 2, 2, hid)
    pos_term = jnp.einsum('skd,ksde->ke', pos_emb, w1)
    h = jax.nn.gelu(g[:, :-1, 0] + g[:, 1:, 1] + pos_term)
    out = jnp.einsum('bnke,ked->bnkd', h, w2)
    return out[:, :, 0], out[:, :, 1]


def slc_importance(p_cmp, n_slc):
    b, lq, n_cmp = p_cmp.shape
    R = SLC_BLOCK // CMP_STRIDE
    pp = jnp.pad(p_cmp, ((0, 0), (0, 0), (1, R * (n_slc + 1) - 1 - n_cmp))).reshape(b, lq, n_slc + 1, R)
    return pp[..., :-1, 0] + 2.0 * jnp.sum(pp[..., :-1, 1:], axis=-1) + pp[..., 1:, 0]


def select_blocks(imp, pos, n_slc):
    j = jnp.arange(n_slc, dtype=jnp.int32)[None, :]
    cur = (pos // SLC_BLOCK)[:, None]
    valid = j <= cur
    forced = (j == 0) | (j >= cur - (N_LOCAL - 1))
    score = jnp.where(valid, imp + jnp.where(forced, FORCE_BONUS, 0.0), -jnp.inf)
    vals, idx = lax.top_k(score, min(SLC_TOPN, n_slc))
    return idx, jnp.isfinite(vals)


def select_mask(imp, pos, n_slc):
    j = jnp.arange(n_slc, dtype=jnp.int32)
    cur = (pos // SLC_BLOCK)[:, None]
    valid = j[None, :] <= cur
    forced = (j[None, :] == 0) | (j[None, :] >= cur - (N_LOCAL - 1))
    score = jnp.where(valid, imp + jnp.where(forced, FORCE_BONUS, 0.0), -jnp.inf)
    mine, other = score[..., :, None], score[..., None, :]
    beats = (other > mine) | ((other == mine) & (j[None, :] < j[:, None]))
    rank = jnp.sum(beats.astype(jnp.int32), axis=-1)
    return (rank < min(SLC_TOPN, n_slc)) & jnp.isfinite(score)


def slc_attend(q, slc_blocks, pos, idx, sel_valid, scale):
    b, lq = q.shape[:2]
    qb = query_block(lq)
    n_sel = idx.shape[-1]
    offs = jnp.arange(SLC_BLOCK, dtype=jnp.int32)

    def one(args):
        qq, pp, ii, vv = args
        g = jax.vmap(lambda blk, ix: blk[ix])(slc_blocks, ii)
        g = g.reshape(b, qb, n_sel * SLC_BLOCK, 2, NSA_D)
        kpos = (ii[..., None] * SLC_BLOCK + offs).reshape(b, qb, n_sel * SLC_BLOCK)
        mask = jnp.repeat(vv, SLC_BLOCK, axis=-1) & (kpos <= pp[None, :, None])
        o, _ = attend_shared_kv(qq[:, :, None], g[:, :, :, 0], g[:, :, :, 1], mask[:, :, None, None, :], scale)
        return o[:, :, 0]

    out = lax.map(one, (split_query_blocks(q, qb), pos.reshape(-1, qb),
                        split_query_blocks(idx, qb), split_query_blocks(sel_valid, qb)))
    return merge_query_blocks(out)


def window_attend_banded(q, k, v, pos, scale):
    b, l = q.shape[:2]
    qb = query_block(l)
    nb = l // qb
    nw = -(-WINDOW // qb)
    pad = lambda t: jnp.pad(t, ((0, 0), (nw * qb, 0), (0, 0))).reshape(b, nw + nb, qb, NSA_D)
    band = jnp.arange(nb)[:, None] + jnp.arange(nw + 1)[None, :]
    kb = pad(k)[:, band].reshape(b, nb, (nw + 1) * qb, NSA_D)
    vb = pad(v)[:, band].reshape(b, nb, (nw + 1) * qb, NSA_D)
    kpos = (band[..., None] * qb + jnp.arange(qb)).reshape(nb, -1) - nw * qb
    dist = pos.reshape(nb, qb)[:, :, None] - kpos[:, None, :]
    mask = (dist >= 0) & (dist < WINDOW) & (kpos[:, None, :] >= 0)
    o, _ = attend_shared_kv(q.reshape(b, nb, qb, NSA_HEADS, NSA_D), kb, vb, mask[None, :, None], scale)
    return o.reshape(b, l, NSA_HEADS, NSA_D)


def nsa_mixer(q, kv, gates, pos, past_cmp, slc_paged, win_buf, g_q, g_k, cmp_pos, cmp_w1, cmp_w2):
    b, l = q.shape[:2]
    scale = NSA_D ** -0.5
    q = rmsnorm(q.reshape(b, l, NSA_HEADS, NSA_D), g_q)
    kv = kv.reshape(b, l, 3, 2, NSA_D)
    cmp_rows = kv[:, :, 0]
    slc_rows = jnp.stack([rmsnorm(kv[:, :, 1, 0], g_k[1]), kv[:, :, 1, 1]], axis=2)
    win_rows = jnp.stack([rmsnorm(kv[:, :, 2, 0], g_k[2]), kv[:, :, 2, 1]], axis=2)
    if past_cmp is None:
        T = l
        kc, vc = compress(cmp_rows, cmp_pos, cmp_w1, cmp_w2)
    else:
        cache_cmp, page_table, layer = past_cmp
        n_past = page_table.shape[1] * cache_cmp.shape[2]
        T = n_past + l
        if (T // CMP_STRIDE) * CMP_STRIDE == n_past:
            kc, vc = compress_paged(cache_cmp, page_table, layer, cmp_pos, cmp_w1, cmp_w2)
        else:
            cmp_all = jnp.concatenate([gather_pages(cache_cmp, layer, page_table), cmp_rows], axis=1)
            kc, vc = compress(cmp_all, cmp_pos, cmp_w1, cmp_w2)
    kc = rmsnorm(kc, g_k[0])
    cmp_end = jnp.arange(kc.shape[1], dtype=jnp.int32) * CMP_STRIDE + CMP_BLOCK - 1
    o_cmp, p_cmp = attend_shared_kv(q, kc, vc, (cmp_end[None, :] <= pos[:, None])[None, None], scale)
    n_slc = -(-T // SLC_BLOCK)
    imp = slc_importance(jnp.sum(p_cmp, axis=1), n_slc)
    if past_cmp is None:
        sel = select_mask(imp, pos, n_slc)
        o_slc = slc_attend_dense(q, slc_rows[:, :, 0], slc_rows[:, :, 1], sel, scale)
    else:
        idx, sel_valid = select_blocks(imp, pos, n_slc)
        cache_slc, page_table, layer = slc_paged
        assert l == 1 and (T - 1) % SLC_BLOCK == 0 and T - 1 == page_table.shape[1] * cache_slc.shape[2]
        o_slc = slc_decode_paged(q[:, 0], slc_rows[:, :, 0], slc_rows[:, :, 1], idx[:, 0], sel_valid[:, 0],
                                 cache_slc, page_table, layer, scale)[:, None]
    if win_buf is None:
        o_win = window_attend_banded(q, win_rows[:, :, 0], win_rows[:, :, 1], pos, scale)
        new_win = win_rows[:, max(l - WINDOW, 0):]
    else:
        wb = win_buf.shape[1]
        rows = jnp.concatenate([win_buf.astype(win_rows.dtype), win_rows], axis=1)
        kpos = jnp.arange(rows.shape[1], dtype=jnp.int32) + (pos[0] - wb)
        dist = pos[:, None] - kpos[None, :]
        mask = (dist >= 0) & (dist < WINDOW)
        o_win, _ = attend_shared_kv(q, rows[:, :, 0], rows[:, :, 1], mask[None, None], scale)
        new_win = rows[:, rows.shape[1] - wb:]
    g = jax.nn.sigmoid(gates.reshape(b, l, 3, NSA_HEADS).astype(F32))[..., None]
    o = g[:, :, 0] * o_cmp + g[:, :, 1] * o_slc + g[:, :, 2] * o_win
    return o.reshape(b, l, NSA_HEADS * NSA_D).astype(q.dtype), cmp_rows, slc_rows, new_win


def mixers(proj, b, l, pos, past, p):
    def sec(name):
        o = _DST_OFF[name]
        return proj[:, o:o + _SEC[name]].reshape(b, l, _SEC[name])

    if past is None:
        mla_paged = past_cmp = slc_paged = win_buf = None
        h0 = jnp.zeros((b, SSM_HEADS, SSM_HEADDIM, SSM_DSTATE), F32)
        conv_buf = jnp.zeros((b, SSM_CONV - 1, SSM_CONV_DIM), F32)
    else:
        mla_paged, past_cmp, slc_paged, win_buf, h0, conv_buf = past
    o_ssm, h_new, conv_new = ssm_mixer(sec('z'), sec('xbc'), sec('dt'), conv_buf, h0, p['ssm_conv_w'],
                                       p['ssm_conv_b'], p['ssm_dt_bias'], p['ssm_a_log'], p['ssm_d'],
                                       p['ssm_g_norm'])
    o_mla, ckv_new, kpe_new = mla_mixer(sec('cq'), sec('ckv'), sec('kpe'), pos, mla_paged,
                                        p['mla_g_q_a'], p['mla_w_uq'], p['mla_g_q'], p['mla_g_kv_a'],
                                        p['mla_g_kpe'], p['mla_w_uk'], p['mla_w_uv'])
    o_nsa, cmp_new, slc_new, win_new = nsa_mixer(sec('nq'), sec('nkv'), sec('ngate'), pos, past_cmp, slc_paged,
                                                 win_buf, p['nsa_g_q'], p['nsa_g_k'], p['nsa_cmp_pos'],
                                                 p['nsa_cmp_w1'], p['nsa_cmp_w2'])
    br = jnp.stack([o_ssm.reshape(b * l, MIX_W), o_mla.reshape(b * l, MIX_W), o_nsa.reshape(b * l, MIX_W)], axis=0)
    return br, (ckv_new, kpe_new, cmp_new, slc_new, win_new, h_new, conv_new)


def _regroup_w_in(w):
    parts = [w[:, _SRC_OFF[n]:_SRC_OFF[n] + _SEC[n]] for n in _DST_ORDER]
    wr = jnp.concatenate(parts, axis=1).astype(BF16)
    return jnp.pad(wr, ((0, 0), (0, IN_COLS_PAD - wr.shape[1])))


def kernel(x_prompt, x_sample, cache_mla_ckv, cache_mla_kpe, cache_nsa_cmp, cache_nsa_slc, state_nsa_win, state_ssm, state_conv, page_table, c_prompt, c_sample, w_ada, b_ada, g_norm_mix, g_norm_ffn, w_in, ssm_conv_w, ssm_conv_b, ssm_dt_bias, ssm_a_log, ssm_d, ssm_g_norm, mla_g_q_a, mla_w_uq, mla_g_q, mla_g_kv_a, mla_g_kpe, mla_w_uk, mla_w_uv, nsa_g_q, nsa_g_k, nsa_cmp_pos, nsa_cmp_w1, nsa_cmp_w2, w_branch, w_out, moe_w_gu, moe_w_down, w_router, router_bias):
    B, L, D = x_prompt.shape
    S = x_sample.shape[0]
    TP = B * L
    T = TP + S
    past_len = page_table.shape[1] * cache_mla_ckv.shape[2]
    pos_p = jnp.arange(L, dtype=jnp.int32)
    pos_s = past_len + jnp.arange(x_sample.shape[1], dtype=jnp.int32)
    TR = 256
    TM = 640
    assert T % TM == 0 and L % TR == 0 and S % 8 == 0

    nb = B + S
    nb_pad = -(-nb // 8) * 8
    c_all = jnp.pad(jax.nn.silu(jnp.concatenate([c_prompt, c_sample], axis=0)), ((0, nb_pad - nb), (0, 0)))
    mods = [matmul(c_all, w_ada, nb_pad, 512, bias=b_ada[i], name='ada', layer=i) for i in range(DEPTH)]

    def mod_parts(i):
        m = mods[i].reshape(nb_pad, 6, D)
        return [(m[:B, k], m[B:nb, k]) for k in range(6)]

    def rows2(xp, xs, dp, ds, ga, g, sc, sh, wr, name):
        op = rowwise(xp, dp, None if ga is None else ga[0], g, None if sc is None else sc[0],
                     None if sh is None else sh[0], wr, L, TR, name + '_p')
        os_ = rowwise(xs, ds, None if ga is None else ga[1], g, None if sc is None else sc[1],
                      None if sh is None else sh[1], wr, 1, S, name + '_s')
        return op, os_

    xp = x_prompt.reshape(TP, D)
    xs = x_sample.reshape(S, D)
    sh1, sc1, ga1, sh2, sc2, ga2 = mod_parts(0)
    (hn_p,), (hn_s,) = rows2(xp, xs, None, None, None, g_norm_mix[0], sc1, sh1, None, 'norm0')
    sp, ss = [], []
    for i in range(DEPTH):
        p = {
            'ssm_conv_w': ssm_conv_w[i], 'ssm_conv_b': ssm_conv_b[i],
            'ssm_dt_bias': ssm_dt_bias[i], 'ssm_a_log': ssm_a_log[i], 'ssm_d': ssm_d[i],
            'ssm_g_norm': ssm_g_norm[i], 'mla_g_q_a': mla_g_q_a[i], 'mla_w_uq': mla_w_uq[i],
            'mla_g_q': mla_g_q[i], 'mla_g_kv_a': mla_g_kv_a[i], 'mla_g_kpe': mla_g_kpe[i],
            'mla_w_uk': mla_w_uk[i], 'mla_w_uv': mla_w_uv[i], 'nsa_g_q': nsa_g_q[i], 'nsa_g_k': nsa_g_k[i],
            'nsa_cmp_pos': nsa_cmp_pos[i], 'nsa_cmp_w1': nsa_cmp_w1[i], 'nsa_cmp_w2': nsa_cmp_w2[i],
        }
        hn = jnp.concatenate([hn_p, hn_s], axis=0)
        proj = matmul(hn, _regroup_w_in(w_in[i]), TM, IN_TN, name='in_proj')
        br_p, st_p = mixers(proj[:TP], B, L, pos_p, None, p)
        past = ((cache_mla_ckv, cache_mla_kpe, page_table, i), (cache_nsa_cmp, page_table, i),
                (cache_nsa_slc, page_table, i), state_nsa_win[i], state_ssm[i], state_conv[i])
        br_s, st_s = mixers(proj[TP:], S, 1, pos_s, past, p)
        sp.append(st_p)
        ss.append(st_s)
        branches = jnp.concatenate([br_p, br_s], axis=1).astype(BF16)
        mixin = branch_mix(branches, w_branch[i].astype(BF16), proj, TM, 512)
        mix = matmul(mixin, w_out[i].astype(BF16), TM, 512, name='out_proj')
        (x1_p, h2_p, lg_p), (x1_s, h2_s, lg_s) = rows2(xp, xs, mix[:TP], mix[TP:], ga1, g_norm_ffn[i], sc2, sh2,
                                                       w_router, 'ffn_norm')
        moe = moe_ffn(jnp.concatenate([h2_p, h2_s], axis=0), jnp.concatenate([lg_p, lg_s], axis=0),
                      router_bias, moe_w_gu[i].astype(BF16), moe_w_down[i].astype(BF16))
        if i + 1 < DEPTH:
            ga2_cur = ga2
            sh1, sc1, ga1, sh2, sc2, ga2 = mod_parts(i + 1)
            (xp, hn_p), (xs, hn_s) = rows2(x1_p, x1_s, moe[:TP], moe[TP:], ga2_cur, g_norm_mix[i + 1], sc1, sh1,
                                           None, 'mix_norm')
        else:
            (xp,), (xs,) = rows2(x1_p, x1_s, moe[:TP], moe[TP:], ga2, None, None, None, None, 'final_add')

    def st(outs, j):
        return jnp.stack([o[j] for o in outs], axis=0)

    return (xp.reshape(B, L, D), xs.reshape(S, 1, D), st(sp, 0), st(ss, 0), st(sp, 1), st(ss, 1), st(sp, 2),
            st(ss, 2), st(sp, 3), st(ss, 3), st(sp, 4), st(ss, 4), st(sp, 5), st(ss, 5), st(sp, 6), st(ss, 6))
```
